```python
import math
import jax, jax.numpy as jnp
from jax import lax
import numpy as np

D_MODEL = 1024
BATCH = 4
SEQ = 4096
DEPTH = 2

D_MIX = D_MODEL
GROUP_WIDTH = D_MIX // 4

GLA_HEADS = 4
GLA_DK = GROUP_WIDTH // (2 * GLA_HEADS)
GLA_DV = GROUP_WIDTH // GLA_HEADS
GLA_GATE_RANK = 16
GLA_GATE_NORM = 16.0
GLA_CHUNK = 64

POOL_WINDOWS = (2, 4, 8, 16)
POOL_GROUPS = 4
POOL_CH = GROUP_WIDTH // POOL_GROUPS

DIFF_HEADS = 4
DIFF_DH = GROUP_WIDTH // (2 * DIFF_HEADS)

MLA_HEADS = 4
MLA_Q_RANK = 192
MLA_KV_RANK = 128
MLA_NOPE = 64
MLA_ROPE = 32
MLA_V = GROUP_WIDTH // MLA_HEADS

ROPE_THETA = 10000.0
Q_BLOCK = 128

N_EXPERTS = 32
TOP_K = 4
D_EXPERT = D_MODEL
SWIGLU_LIMIT = 7.0
SWIGLU_ALPHA = 1.702
MOE_BLOCK = 128

NORM_EPS = 1e-5
DEEPNORM_ALPHA = (2 * DEPTH) ** 0.25
DEEPNORM_BETA = (8 * DEPTH) ** -0.25

SPLIT_SIZES = (
    GLA_HEADS * GLA_DK, GLA_HEADS * GLA_DK, GLA_HEADS * GLA_DV, GLA_HEADS * GLA_DV, GLA_GATE_RANK,
    GROUP_WIDTH,
    DIFF_HEADS * 2 * DIFF_DH, DIFF_HEADS * 2 * DIFF_DH, DIFF_HEADS * 2 * DIFF_DH,
    MLA_Q_RANK, MLA_KV_RANK + MLA_ROPE,
)
D_IN = sum(SPLIT_SIZES)

F32 = jnp.float32

kernel_name = 'hybrid_gla_pool_diff_mla_moe_deepnorm'


def rms_norm(x, g, eps=1e-6):
    xf = x.astype(F32)
    return xf * lax.rsqrt(jnp.mean(xf * xf, axis=-1, keepdims=True) + eps) * g.astype(F32)


def layer_norm(x, g, b):
    xf = x.astype(F32)
    mu = jnp.mean(xf, axis=-1, keepdims=True)
    var = jnp.mean(jnp.square(xf - mu), axis=-1, keepdims=True)
    return ((xf - mu) * lax.rsqrt(var + NORM_EPS) * g.astype(F32) + b.astype(F32)).astype(x.dtype)


def rope_tables(positions, dim):
    inv = 1.0 / (ROPE_THETA ** (jnp.arange(0, dim, 2, dtype=F32) / dim))
    ang = positions.astype(F32)[..., None] * inv
    return jnp.cos(ang), jnp.sin(ang)


def apply_rope(x, cos, sin):
    half = x.shape[-1] // 2
    xf = x.astype(F32)
    x1, x2 = xf[..., :half], xf[..., half:]
    shape = cos.shape[:2] + (1,) * (x.ndim - 3) + (half,)
    c, s = cos.reshape(shape), sin.reshape(shape)
    return jnp.concatenate([x1 * c - x2 * s, x2 * c + x1 * s], axis=-1).astype(x.dtype)


def gla_chunk_scan(q, k, v, log_a):
    B, H, S, dk = q.shape
    dv = v.shape[-1]
    n = S // GLA_CHUNK

    def chunks(t):
        return t.reshape(B, H, n, GLA_CHUNK, t.shape[-1]).transpose(2, 0, 1, 3, 4)

    idx = jnp.arange(GLA_CHUNK)
    causal = (idx[:, None] >= idx[None, :])[:, :, None]

    def step(state, inp):
        qc, kc, vc, ac = inp
        b = jnp.cumsum(ac, axis=2)
        rel = b[:, :, :, None, :] - b[:, :, None, :, :]
        decay = jnp.exp(jnp.where(causal, rel, -jnp.inf))
        scores = jnp.einsum('bhid,bhjd,bhijd->bhij', qc, kc, decay)
        out = (jnp.einsum('bhij,bhje->bhie', scores, vc)
               + jnp.einsum('bhid,bhde->bhie', qc * jnp.exp(b), state))
        b_last = b[:, :, -1:, :]
        state = (jnp.exp(b_last[:, :, 0, :, None]) * state
                 + jnp.einsum('bhjd,bhje->bhde', kc * jnp.exp(b_last - b), vc))
        return state, out

    state0 = jnp.zeros((B, H, dk, dv), F32)
    _, o = lax.scan(step, state0, (chunks(q), chunks(k), chunks(v), chunks(log_a)))
    return o.transpose(1, 2, 0, 3, 4).reshape(B, H, S, dv)


def multiscale_pool(u, w_pool, pool_scale):
    B, S, _ = u.shape
    uf = u.astype(F32).reshape(B, S, POOL_GROUPS, POOL_CH)
    cs = jnp.concatenate([jnp.zeros((B, 1, POOL_GROUPS, POOL_CH), F32),
                          jnp.cumsum(uf, axis=1)], axis=1)
    t = jnp.arange(S)
    means = []
    for g, w in enumerate(POOL_WINDOWS):
        csg = cs[:, :, g]
        prev = jnp.concatenate([jnp.zeros((B, w - 1, POOL_CH), F32), csg[:, :S - w + 1]], axis=1)
        count = jnp.minimum(t + 1, w).astype(F32)[None, :, None]
        means.append((csg[:, 1:] - prev) / count)
    pooled = jnp.stack(means, axis=2) - uf
    y = jnp.einsum('bsgc,gcd->bsgd', pooled, w_pool.astype(F32)).reshape(B, S, GROUP_WIDTH)
    return y * pool_scale.astype(F32)


def _block_probs(qb, k, q0, scale):
    s = jnp.einsum('bhqd,bhkd->bhqk', qb, k).astype(F32) * scale
    qpos = q0 + jnp.arange(qb.shape[2])
    kpos = jnp.arange(k.shape[2])
    s = jnp.where(kpos[None, :] <= qpos[:, None], s, -jnp.inf)
    return jax.nn.softmax(s, axis=-1)


def _merge_blocks(o):
    nb, B, H, qb, d = o.shape
    return o.transpose(1, 0, 3, 2, 4).reshape(B, nb * qb, H, d)


def causal_attention(q, k, v, scale):
    S = q.shape[2]

    def one(i):
        q0 = i * Q_BLOCK
        p = _block_probs(lax.dynamic_slice_in_dim(q, q0, Q_BLOCK, axis=2), k, q0, scale)
        return jnp.einsum('bhqk,bhkd->bhqd', p.astype(v.dtype), v)

    return _merge_blocks(lax.map(one, jnp.arange(S // Q_BLOCK)))


def diff_attention(q1, q2, k1, k2, v, lam):
    S, dh = q1.shape[2], q1.shape[3]
    scale = dh ** -0.5

    def one(i):
        q0 = i * Q_BLOCK
        p1 = _block_probs(lax.dynamic_slice_in_dim(q1, q0, Q_BLOCK, axis=2), k1, q0, scale)
        p2 = _block_probs(lax.dynamic_slice_in_dim(q2, q0, Q_BLOCK, axis=2), k2, q0, scale)
        return jnp.einsum('bhqk,bhkd->bhqd', (p1 - lam * p2).astype(v.dtype), v)

    return _merge_blocks(lax.map(one, jnp.arange(S // Q_BLOCK)))


def hybrid_mixer(x, cos, sin, layer, w_in, gla_w_gate, gla_b_gate, gla_norm, pool_w, pool_scale,
                 diff_lq1, diff_lk1, diff_lq2, diff_lk2, diff_norm,
                 mla_q_norm, mla_w_uq, mla_kv_norm, mla_w_ukv, w_out):
    B, S, _ = x.shape
    dt = x.dtype
    proj = jnp.dot(x, w_in)
    offs = np.cumsum(SPLIT_SIZES)[:-1].tolist()
    (g_q, g_k, g_v, g_g, g_r, p_u, d_q, d_k, d_v, m_qa, m_kva) = jnp.split(proj, offs, axis=-1)

    def heads(t, h):
        return t.reshape(B, S, h, -1).transpose(0, 2, 1, 3)

    q = heads(g_q.astype(F32), GLA_HEADS) * GLA_DK ** -0.5
    k = heads(g_k.astype(F32), GLA_HEADS)
    v = heads(g_v.astype(F32), GLA_HEADS)
    gate_logit = (jnp.dot(g_r, gla_w_gate) + gla_b_gate).astype(F32)
    log_a = heads(jax.nn.log_sigmoid(gate_logit) / GLA_GATE_NORM, GLA_HEADS)
    o = gla_chunk_scan(q, k, v, log_a).transpose(0, 2, 1, 3)
    o = rms_norm(o, gla_norm) * jax.nn.silu(g_g.astype(F32).reshape(B, S, GLA_HEADS, GLA_DV))
    gla_out = o.reshape(B, S, GROUP_WIDTH).astype(dt)

    pool_out = multiscale_pool(p_u, pool_w, pool_scale).astype(dt)

    lam_init = 0.8 - 0.6 * math.exp(-0.3 * layer)
    lam = (jnp.exp(jnp.sum(diff_lq1.astype(F32) * diff_lk1.astype(F32)))
           - jnp.exp(jnp.sum(diff_lq2.astype(F32) * diff_lk2.astype(F32))) + lam_init)
    cq = apply_rope(d_q.reshape(B, S, DIFF_HEADS, 2, DIFF_DH), cos, sin).transpose(0, 2, 3, 1, 4)
    ck = apply_rope(d_k.reshape(B, S, DIFF_HEADS, 2, DIFF_DH), cos, sin).transpose(0, 2, 3, 1, 4)
    cv = d_v.reshape(B, S, DIFF_HEADS, 2 * DIFF_DH).transpose(0, 2, 1, 3)
    o = diff_attention(cq[:, :, 0], cq[:, :, 1], ck[:, :, 0], ck[:, :, 1], cv, lam)
    diff_out = (rms_norm(o, diff_norm, 1e-5) * (1.0 - lam_init)).reshape(B, S, GROUP_WIDTH).astype(dt)

    c_q = rms_norm(m_qa, mla_q_norm).astype(dt)
    mq = jnp.dot(c_q, mla_w_uq).reshape(B, S, MLA_HEADS, MLA_NOPE + MLA_ROPE)
    mq = jnp.concatenate([mq[..., :MLA_NOPE], apply_rope(mq[..., MLA_NOPE:], cos, sin)], axis=-1)
    c_kv = rms_norm(m_kva[..., :MLA_KV_RANK], mla_kv_norm).astype(dt)
    k_rope = apply_rope(m_kva[..., MLA_KV_RANK:][:, :, None, :], cos, sin)
    kv = jnp.dot(c_kv, mla_w_ukv).reshape(B, S, MLA_HEADS, MLA_NOPE + MLA_V)
    mk = jnp.concatenate([kv[..., :MLA_NOPE],
                          jnp.broadcast_to(k_rope, (B, S, MLA_HEADS, MLA_ROPE))], axis=-1)
    mv = kv[..., MLA_NOPE:]
    o = causal_attention(mq.transpose(0, 2, 1, 3), mk.transpose(0, 2, 1, 3),
                         mv.transpose(0, 2, 1, 3), (MLA_NOPE + MLA_ROPE) ** -0.5)
    mla_out = o.reshape(B, S, GROUP_WIDTH).astype(dt)

    mixed = jnp.concatenate([gla_out, pool_out, diff_out, mla_out], axis=-1)
    return jnp.dot(mixed, w_out)


def clamped_swiglu(gu):
    gu = gu.astype(F32)
    gate, up = jnp.split(gu, 2, axis=-1)
    gate = jnp.minimum(gate, SWIGLU_LIMIT)
    up = jnp.clip(up, -SWIGLU_LIMIT, SWIGLU_LIMIT)
    return (up + 1.0) * gate * jax.nn.sigmoid(SWIGLU_ALPHA * gate)


def moe_ffn(h, router_w, router_b, w_gate_up, b_gate_up, w_down, b_down):
    B, S, D = h.shape
    T = B * S
    M = T * TOP_K
    ht = h.reshape(T, D)
    logits = jnp.dot(ht, router_w).astype(F32) + router_b.astype(F32)
    top_logit, top_idx = lax.top_k(logits, TOP_K)
    gates = jax.nn.softmax(top_logit, axis=-1)

    flat_e = top_idx.reshape(M)
    flat_tok = jnp.repeat(jnp.arange(T, dtype=jnp.int32), TOP_K)
    order = jnp.argsort(flat_e, stable=True)
    sorted_e = flat_e[order]
    counts = jnp.bincount(flat_e, length=N_EXPERTS)
    padded = (counts + MOE_BLOCK - 1) // MOE_BLOCK * MOE_BLOCK
    pad_end = jnp.cumsum(padded)
    pad_start = pad_end - padded
    start = jnp.cumsum(counts) - counts
    dest_sorted = pad_start[sorted_e] + (jnp.arange(M) - start[sorted_e])
    n_blocks = -(-M // MOE_BLOCK) + N_EXPERTS
    n_rows = n_blocks * MOE_BLOCK
    row_tok = jnp.full((n_rows,), T, jnp.int32).at[dest_sorted].set(flat_tok[order])
    h_pad = jnp.concatenate([ht, jnp.zeros((1, D), ht.dtype)], axis=0)
    xs = h_pad[row_tok].reshape(n_blocks, MOE_BLOCK, D)
    block_e = jnp.minimum(jnp.searchsorted(pad_end, jnp.arange(n_blocks) * MOE_BLOCK, side='right'),
                          N_EXPERTS - 1)

    def expert_block(args):
        xb, e = args
        gu = jnp.dot(xb, w_gate_up[e]) + b_gate_up[e]
        y = clamped_swiglu(gu).astype(xb.dtype)
        return jnp.dot(y, w_down[e]) + b_down[e]

    ys = lax.map(expert_block, (xs, block_e)).reshape(n_rows, D)
    dest = jnp.zeros((M,), dest_sorted.dtype).at[order].set(dest_sorted)
    y_assign = ys[dest].reshape(T, TOP_K, D).astype(F32)
    out = jnp.einsum('tkd,tk->td', y_assign, gates)
    return out.astype(h.dtype).reshape(B, S, D)


def setup_inputs(seed: int = 0) -> dict:
    key = jax.random.key(seed)
    ks = iter(jax.random.split(key, 40))
    L = DEPTH

    def nrm(shape, scale):
        return jax.random.normal(next(ks), shape, F32) * scale

    def gain(shape):
        return 1.0 + nrm(shape, 0.02)

    x = nrm((BATCH, SEQ, D_MODEL), 1.0)
    offset = jax.random.randint(next(ks), (BATCH, 1), 0, 1024, dtype=jnp.int32)
    positions = offset + jnp.arange(SEQ, dtype=jnp.int32)[None, :]
    return {
        'x': x,
        'positions': positions,
        'w_in': nrm((L, D_MODEL, D_IN), D_MODEL ** -0.5),
        'gla_w_gate': nrm((L, GLA_GATE_RANK, GLA_HEADS * GLA_DK), GLA_GATE_RANK ** -0.5),
        'gla_b_gate': nrm((L, GLA_HEADS * GLA_DK), 0.1),
        'gla_norm': gain((L, GLA_DV)),
        'pool_w': nrm((L, POOL_GROUPS, POOL_CH, POOL_CH), POOL_CH ** -0.5),
        'pool_scale': gain((L, GROUP_WIDTH)),
        'diff_lq1': nrm((L, DIFF_DH), 0.1),
        'diff_lk1': nrm((L, DIFF_DH), 0.1),
        'diff_lq2': nrm((L, DIFF_DH), 0.1),
        'diff_lk2': nrm((L, DIFF_DH), 0.1),
        'diff_norm': gain((L, 2 * DIFF_DH)),
        'mla_q_norm': gain((L, MLA_Q_RANK)),
        'mla_w_uq': nrm((L, MLA_Q_RANK, MLA_HEADS * (MLA_NOPE + MLA_ROPE)), MLA_Q_RANK ** -0.5),
        'mla_kv_norm': gain((L, MLA_KV_RANK)),
        'mla_w_ukv': nrm((L, MLA_KV_RANK, MLA_HEADS * (MLA_NOPE + MLA_V)), MLA_KV_RANK ** -0.5),
        'w_out': nrm((L, D_MIX, D_MODEL), D_MIX ** -0.5 * DEEPNORM_BETA),
        'ln1_g': gain((L, D_MODEL)),
        'ln1_b': nrm((L, D_MODEL), 0.02),
        'router_w': nrm((L, D_MODEL, N_EXPERTS), D_MODEL ** -0.5),
        'router_b': nrm((L, N_EXPERTS), 0.01),
        'w_gate_up': nrm((L, N_EXPERTS, D_MODEL, 2 * D_EXPERT), D_MODEL ** -0.5),
        'b_gate_up': nrm((L, N_EXPERTS, 2 * D_EXPERT), 0.01),
        'w_down': nrm((L, N_EXPERTS, D_EXPERT, D_MODEL), D_EXPERT ** -0.5 * DEEPNORM_BETA),
        'b_down': nrm((L, N_EXPERTS, D_MODEL), 0.01),
        'ln2_g': gain((L, D_MODEL)),
        'ln2_b': nrm((L, D_MODEL), 0.02),
    }


def reference(x, positions, w_in, gla_w_gate, gla_b_gate, gla_norm, pool_w, pool_scale,
              diff_lq1, diff_lk1, diff_lq2, diff_lk2, diff_norm,
              mla_q_norm, mla_w_uq, mla_kv_norm, mla_w_ukv, w_out, ln1_g, ln1_b,
              router_w, router_b, w_gate_up, b_gate_up, w_down, b_down, ln2_g, ln2_b):
    cos, sin = rope_tables(positions, DIFF_DH)
    h = x
    for l in range(DEPTH):
        a = hybrid_mixer(h, cos, sin, l, w_in[l], gla_w_gate[l], gla_b_gate[l], gla_norm[l],
                         pool_w[l], pool_scale[l], diff_lq1[l], diff_lk1[l], diff_lq2[l], diff_lk2[l],
                         diff_norm[l], mla_q_norm[l], mla_w_uq[l], mla_kv_norm[l], mla_w_ukv[l], w_out[l])
        h = layer_norm(DEEPNORM_ALPHA * h + a, ln1_g[l], ln1_b[l])
        f = moe_ffn(h, router_w[l], router_b[l], w_gate_up[l], b_gate_up[l], w_down[l], b_down[l])
        h = layer_norm(DEEPNORM_ALPHA * h + f, ln2_g[l], ln2_b[l])
    return h
```

```python
import functools
import math

import jax
import jax.numpy as jnp
from jax import lax
from jax.experimental import pallas as pl
from jax.experimental.pallas import tpu as pltpu

F32 = jnp.float32
BF16 = jnp.bfloat16
I32 = jnp.int32

D_MODEL = 1024
DEPTH = 2
GROUP_WIDTH = 256
N_HEADS = 4
GLA_DK = 32
GLA_DV = 64
GLA_GATE_RANK = 16
GLA_GATE_NORM = 16.0
POOL_WINDOWS = (2, 4, 8, 16)
POOL_CH = 64
DIFF_DH = 32
MLA_Q_RANK = 192
MLA_KV_RANK = 128
MLA_NOPE = 64
MLA_ROPE = 32
MLA_V = 64
ROPE_THETA = 10000.0
N_EXPERTS = 32
TOP_K = 4
D_EXPERT = 1024
SWIGLU_LIMIT = 7.0
SWIGLU_ALPHA = 1.702
NORM_EPS = 1e-5
DEEPNORM_ALPHA = (2 * DEPTH) ** 0.25

LANES = 128
MXU_WIDTH = 256
VMEM_LIMIT = 56 * 1024 * 1024

C_GLA = 0
C_POOL = 896
C_DQ = 1152
C_DK = 1408
C_DV = 1664
C_MQA = 1920
C_MKVA = 2176
C_END = 2432
GLA_W = C_POOL - C_GLA

NEG = -1e30

TM_PROJ = 512
TG_GLA = 512
GLA_CHUNK = 64
TQ = 512
TM_OUT = 512
BM_MOE = 256
HC_MOE = 512
TM_COMB = 256


def _cparams(sem):
    return pltpu.CompilerParams(dimension_semantics=sem, vmem_limit_bytes=VMEM_LIMIT)


def _lane_iota(shape):
    return lax.broadcasted_iota(I32, shape, len(shape) - 1)


def _row_iota(shape):
    return lax.broadcasted_iota(I32, shape, len(shape) - 2)


def _rope(x, cos, sin_signed):
    w = x.shape[-1]
    from_low = pltpu.roll(x, 16, axis=1)
    from_high = pltpu.roll(x, w - 16, axis=1)
    swapped = jnp.where((_lane_iota(x.shape) % 32) >= 16, from_low, from_high)
    return x * cos + swapped * sin_signed


def _proj_kernel(x_ref, w_ref, cos_ref, sin_ref, qn_ref, kvn_ref, wuq_ref, wukv_ref, sel_ref,
                 gla_ref, pool_ref, dq_ref, dk_ref, dv_ref, mq_ref, mk_ref, mv_ref):
    xb = x_ref[...].astype(BF16)

    def seg(lo, hi):
        return jnp.dot(xb, w_ref[:, lo:hi], preferred_element_type=F32)

    gla_ref[...] = seg(C_GLA, C_POOL)
    pool_ref[...] = seg(C_POOL, C_DQ)

    cos = cos_ref[...]
    sin = sin_ref[...]
    cos2 = jnp.concatenate([cos, cos], axis=1)
    sin2 = jnp.concatenate([sin, sin], axis=1)
    dq_ref[...] = (_rope(seg(C_DQ, C_DK), cos2, sin2) * (DIFF_DH ** -0.5)).astype(BF16)
    dk_ref[...] = _rope(seg(C_DK, C_DV), cos2, sin2).astype(BF16)
    dv_ref[...] = seg(C_DV, C_MQA).astype(BF16)

    mqa = seg(C_MQA, C_MKVA)
    ms = jnp.sum(mqa * mqa, axis=1, keepdims=True) * (1.0 / MLA_Q_RANK)
    cq = (mqa * lax.rsqrt(ms + 1e-6) * qn_ref[...]).astype(BF16)
    mq = jnp.dot(cq, wuq_ref[...], preferred_element_type=F32)
    cos4 = jnp.concatenate([cos2, cos2], axis=1)
    sin4 = jnp.concatenate([sin2, sin2], axis=1)
    lane = _lane_iota(mq.shape) % LANES
    is_rope = (lane >= MLA_NOPE) & (lane < MLA_NOPE + MLA_ROPE)
    mq = jnp.where(is_rope, _rope(mq, cos4, sin4), mq)
    mq_ref[...] = (mq * ((MLA_NOPE + MLA_ROPE) ** -0.5)).astype(BF16)

    mkva = seg(C_MKVA, C_END)
    ckv_raw = mkva[:, :LANES]
    msk = jnp.sum(ckv_raw * ckv_raw, axis=1, keepdims=True) * (1.0 / MLA_KV_RANK)
    ckv = (ckv_raw * lax.rsqrt(msk + 1e-6) * kvn_ref[...]).astype(BF16)
    kv = jnp.dot(ckv, wukv_ref[...], preferred_element_type=F32)
    k_rope = _rope(mkva[:, LANES:], cos, sin).astype(BF16)
    placed = jnp.dot(k_rope, sel_ref[...], preferred_element_type=F32)
    mk_ref[...] = (kv[:, :4 * LANES] + placed).astype(BF16)
    mv_ref[...] = kv[:, 4 * LANES:].astype(BF16)


def _proj(x2, w_big, cos128, sin128, qn, kvn, wuq, wukv, sel):
    t = x2.shape[0]
    tm = TM_PROJ
    row = lambda w: pl.BlockSpec((tm, w), lambda i: (i, 0))
    full = lambda a: pl.BlockSpec(a.shape, lambda i: (0,) * a.ndim)
    out_shapes = [
        jax.ShapeDtypeStruct((t, GLA_W), F32), jax.ShapeDtypeStruct((t, 256), F32),
        jax.ShapeDtypeStruct((t, 256), BF16), jax.ShapeDtypeStruct((t, 256), BF16),
        jax.ShapeDtypeStruct((t, 256), BF16), jax.ShapeDtypeStruct((t, 512), BF16),
        jax.ShapeDtypeStruct((t, 512), BF16), jax.ShapeDtypeStruct((t, 256), BF16)]
    return pl.pallas_call(
        _proj_kernel,
        grid=(t // tm,),
        in_specs=[row(D_MODEL), full(w_big), row(LANES), row(LANES), full(qn), full(kvn),
                  full(wuq), full(wukv), full(sel)],
        out_specs=[row(s.shape[1]) for s in out_shapes],
        out_shape=out_shapes,
        compiler_params=_cparams(("parallel",)),
    )(x2, w_big, cos128, sin128, qn, kvn, wuq, wukv, sel)


def _split3(x):
    hi = x.astype(BF16)
    r1 = x - hi.astype(F32)
    mid = r1.astype(BF16)
    lo = (r1 - mid.astype(F32)).astype(BF16)
    return hi, mid, lo


def _gla_kernel(blk_ref, wg_ref, bg_ref, norm_ref, ind_ref, tri_ref, out_ref, st_ref, *, n_chunks):
    c_len = GLA_CHUNK

    @pl.when(pl.program_id(1) == 0)
    def _():
        st_ref[...] = jnp.zeros_like(st_ref)

    same_head = (_row_iota((256, LANES)) // GLA_DV) == (_lane_iota((256, LANES)) // GLA_DK)
    ind = ind_ref[...]
    tri = tri_ref[...]
    rows = _row_iota((c_len, LANES))

    def chunk(c, carry):
        r0 = pl.multiple_of(c * c_len, c_len)
        blk = blk_ref[pl.ds(r0, c_len), :]
        q = blk[:, 0:128] * (GLA_DK ** -0.5)
        k = blk[:, 128:256]
        v = blk[:, 256:512]
        g = blk[:, 512:768]
        gr = blk[:, 768:896]
        logit = jnp.dot(gr.astype(BF16), wg_ref[...], preferred_element_type=F32) + bg_ref[...]
        log_a = (jnp.minimum(logit, 0.0) - jnp.log(1.0 + jnp.exp(-jnp.abs(logit)))) * (1.0 / GLA_GATE_NORM)
        hi, mid, lo = _split3(log_a)
        b = (jnp.dot(tri, hi, preferred_element_type=F32) + jnp.dot(tri, mid, preferred_element_type=F32)
             + jnp.dot(tri, lo, preferred_element_type=F32))

        o_tiles = [jnp.zeros((8, 256), F32) for _ in range(c_len // 8)]
        for j in range(c_len):
            t0 = j // 8
            i0 = 8 * t0
            e = jnp.exp(b[i0:, :] - b[j:j + 1, :]) * (q[i0:, :] * k[j:j + 1, :])
            e = jnp.where(rows[i0:, :] >= j, e, 0.0)
            p = jnp.dot(e.astype(BF16), ind, preferred_element_type=F32)
            pv = p * v[j:j + 1, :]
            for t in range(t0, c_len // 8):
                o_tiles[t] = o_tiles[t] + pv[8 * (t - t0):8 * (t - t0 + 1), :]
        o = jnp.concatenate(o_tiles, axis=0)

        st = st_ref[...]
        qd = (q * jnp.exp(b)).astype(BF16)
        o = o + lax.dot_general(qd, st.astype(BF16), (((1,), (1,)), ((), ())), preferred_element_type=F32)
        b_last = b[c_len - 1:c_len, :]
        kd = (k * jnp.exp(b_last - b)).astype(BF16)
        upd = lax.dot_general(v.astype(BF16), kd, (((0,), (0,)), ((), ())), preferred_element_type=F32)
        st_ref[...] = st * jnp.exp(b_last) + jnp.where(same_head, upd, 0.0)

        o2 = o * o
        head = _lane_iota(o.shape) // GLA_DV
        scale = jnp.zeros_like(o)
        for h in range(N_HEADS):
            ms = jnp.sum(jnp.where(head == h, o2, 0.0), axis=1, keepdims=True) * (1.0 / GLA_DV)
            scale = jnp.where(head == h, lax.rsqrt(ms + 1e-6), scale)
        silu = g / (1.0 + jnp.exp(-g))
        out_ref[pl.ds(r0, c_len), :] = (o * scale * norm_ref[...] * silu).astype(out_ref.dtype)
        return carry

    lax.fori_loop(0, n_chunks, chunk, 0)


def _gla(gla_in, wg, bg, norm, ind, tri, batch):
    t = gla_in.shape[0]
    s = t // batch
    tg = TG_GLA
    nblk = s // tg
    full = lambda a: pl.BlockSpec(a.shape, lambda b, i: (0,) * a.ndim)
    return pl.pallas_call(
        functools.partial(_gla_kernel, n_chunks=tg // GLA_CHUNK),
        grid=(batch, nblk),
        in_specs=[pl.BlockSpec((tg, GLA_W), lambda b, i: (b * nblk + i, 0)),
                  full(wg), full(bg), full(norm), full(ind), full(tri)],
        out_specs=pl.BlockSpec((tg, 256), lambda b, i: (b * nblk + i, 0)),
        out_shape=jax.ShapeDtypeStruct((t, 256), BF16),
        scratch_shapes=[pltpu.VMEM((256, LANES), F32)],
        compiler_params=_cparams(("parallel", "arbitrary")),
    )(gla_in, wg, bg, norm, ind, tri)


def _pool_kernel(u_ref, w_ref, scale_ref, out_ref):
    u = u_ref[...]
    t = _row_iota(u.shape)

    def shifted(x, k):
        return jnp.where(t >= k, pltpu.roll(x, k, axis=0), 0.0)

    sums = []
    acc = u
    for k in (1, 2, 4, 8):
        acc = acc + shifted(acc, k)
        sums.append(acc)
    group = _lane_iota(u.shape) // POOL_CH
    tf = (t + 1).astype(F32)
    mean = jnp.zeros_like(u)
    for gi, w in enumerate(POOL_WINDOWS):
        mean = jnp.where(group == gi, sums[gi] / jnp.minimum(tf, float(w)), mean)
    pooled = (mean - u).astype(BF16)
    y = jnp.dot(pooled, w_ref[...], preferred_element_type=F32) * scale_ref[...]
    out_ref[...] = y.astype(out_ref.dtype)


def _pool(pool_in, w_bd, scale, batch):
    t = pool_in.shape[0]
    s = t // batch
    full = lambda a: pl.BlockSpec(a.shape, lambda b: (0,) * a.ndim)
    return pl.pallas_call(
        _pool_kernel,
        grid=(batch,),
        in_specs=[pl.BlockSpec((s, 256), lambda b: (b, 0)), full(w_bd), full(scale)],
        out_specs=pl.BlockSpec((s, 256), lambda b: (b, 0)),
        out_shape=jax.ShapeDtypeStruct((t, 256), BF16),
        compiler_params=_cparams(("parallel",)),
    )(pool_in, w_bd, scale)


def _flash_kernel(qi_tab, ki_tab, q_ref, k_ref, v_ref, aux_ref, o_ref, qexp, m_s, l_s, acc_s,
                  *, heads, diff_lam_init):
    step = pl.program_id(1)
    qi = qi_tab[step]
    ki = ki_tab[step]
    tq = q_ref.shape[0]
    tk = k_ref.shape[0]
    dv = GROUP_WIDTH // N_HEADS

    @pl.when(ki == 0)
    def _():
        for vh, (grp, lo, width, _) in enumerate(heads):
            qg = q_ref[:, grp * MXU_WIDTH:(grp + 1) * MXU_WIDTH]
            lane = _lane_iota(qg.shape)
            qexp[vh] = jnp.where((lane >= lo) & (lane < lo + width), qg, jnp.zeros_like(qg))
        m_s[...] = jnp.full_like(m_s, NEG)
        l_s[...] = jnp.zeros_like(l_s)
        acc_s[...] = jnp.zeros_like(acc_s)

    def update(masked):
        if masked:
            keep = _lane_iota((tq, tk)) <= _row_iota((tq, tk))
        for vh, (grp, _, _, hv) in enumerate(heads):
            kg = k_ref[:, grp * MXU_WIDTH:(grp + 1) * MXU_WIDTH]
            s = lax.dot_general(qexp[vh], kg, (((1,), (1,)), ((), ())), preferred_element_type=F32)
            if masked:
                s = jnp.where(keep, s, NEG)
            m_old = m_s[vh]
            m_new = jnp.maximum(m_old, jnp.max(s, axis=1, keepdims=True))
            alpha = jnp.exp(m_old - m_new)
            p = jnp.exp(s - m_new)
            l_s[vh] = alpha * l_s[vh] + jnp.sum(p, axis=1, keepdims=True)
            pv = jnp.dot(p.astype(BF16), v_ref[:, hv * dv:(hv + 1) * dv], preferred_element_type=F32)
            acc_s[vh] = alpha * acc_s[vh] + pv
            m_s[vh] = m_new

    @pl.when(ki < qi)
    def _():
        update(False)

    @pl.when(ki == qi)
    def _():
        update(True)
        outs = []
        if diff_lam_init is None:
            for vh in range(len(heads)):
                outs.append(acc_s[vh] / l_s[vh])
        else:
            aux = aux_ref[...]
            lam = (jnp.exp(jnp.sum(aux[0:1] * aux[1:2], axis=1, keepdims=True))
                   - jnp.exp(jnp.sum(aux[2:3] * aux[3:4], axis=1, keepdims=True)) + diff_lam_init)
            gain = aux[4:5, :dv]
            for h in range(N_HEADS):
                o = acc_s[2 * h] / l_s[2 * h] - lam * (acc_s[2 * h + 1] / l_s[2 * h + 1])
                ms = jnp.mean(o * o, axis=1, keepdims=True)
                outs.append(o * lax.rsqrt(ms + 1e-5) * gain * (1.0 - diff_lam_init))
        o_ref[...] = jnp.concatenate(outs, axis=1).astype(o_ref.dtype)


def _flash(q, k, v, aux, batch, heads, diff_lam_init):
    t, wq = q.shape
    s = t // batch
    tq = TQ
    nq = s // tq
    pairs = [(a, b) for a in range(nq) for b in range(a + 1)]
    qi_tab = jnp.asarray([p[0] for p in pairs], I32)
    ki_tab = jnp.asarray([p[1] for p in pairs], I32)
    nv = len(heads)
    dv = GROUP_WIDTH // N_HEADS
    grid_spec = pltpu.PrefetchScalarGridSpec(
        num_scalar_prefetch=2,
        grid=(batch, len(pairs)),
        in_specs=[pl.BlockSpec((tq, wq), lambda b, i, qt, kt: (b * nq + qt[i], 0)),
                  pl.BlockSpec((tq, wq), lambda b, i, qt, kt: (b * nq + kt[i], 0)),
                  pl.BlockSpec((tq, 256), lambda b, i, qt, kt: (b * nq + kt[i], 0)),
                  pl.BlockSpec(aux.shape, lambda b, i, qt, kt: (0, 0))],
        out_specs=pl.BlockSpec((tq, 256), lambda b, i, qt, kt: (b * nq + qt[i], 0)),
        scratch_shapes=[pltpu.VMEM((nv, tq, MXU_WIDTH), BF16), pltpu.VMEM((nv, tq, 1), F32),
                        pltpu.VMEM((nv, tq, 1), F32), pltpu.VMEM((nv, tq, dv), F32)])
    return pl.pallas_call(
        functools.partial(_flash_kernel, heads=heads, diff_lam_init=diff_lam_init),
        grid_spec=grid_spec,
        out_shape=jax.ShapeDtypeStruct((t, 256), BF16),
        compiler_params=_cparams(("parallel", "arbitrary")),
    )(qi_tab, ki_tab, q, k, v, aux)


DIFF_HEADS_SPEC = tuple((0, 32 * (2 * h + m), 32, h) for h in range(N_HEADS) for m in range(2))
MLA_HEADS_SPEC = tuple((h // 2, LANES * (h % 2), LANES, h) for h in range(N_HEADS))


def _layer_norm(z, g, b):
    mu = jnp.mean(z, axis=1, keepdims=True)
    zc = z - mu
    var = jnp.mean(zc * zc, axis=1, keepdims=True)
    return zc * lax.rsqrt(var + NORM_EPS) * g + b


def _outproj_kernel(a_ref, p_ref, c_ref, d_ref, x_ref, w_ref, g_ref, b_ref, rwh_ref, rwl_ref, rb_ref,
                    h_ref, idx_ref, gate_ref):
    y = jnp.zeros(h_ref.shape, F32)
    for i, part in enumerate((a_ref, p_ref, c_ref, d_ref)):
        y = y + jnp.dot(part[...], w_ref[i * GROUP_WIDTH:(i + 1) * GROUP_WIDTH, :],
                        preferred_element_type=F32)
    h = _layer_norm(DEEPNORM_ALPHA * x_ref[...] + y, g_ref[...], b_ref[...])
    h_ref[...] = h

    hi = h.astype(BF16)
    lo = (h - hi.astype(F32)).astype(BF16)
    logits = (jnp.dot(hi, rwh_ref[...], preferred_element_type=F32)
              + jnp.dot(lo, rwh_ref[...], preferred_element_type=F32)
              + jnp.dot(hi, rwl_ref[...], preferred_element_type=F32) + rb_ref[...])
    lane = _lane_iota(logits.shape)
    logits = jnp.where(lane < N_EXPERTS, logits, NEG)
    idx_out = jnp.zeros(logits.shape, I32)
    top = []
    for r in range(TOP_K):
        m = jnp.max(logits, axis=1, keepdims=True)
        idx = jnp.min(jnp.where(logits == m, lane, LANES), axis=1, keepdims=True)
        idx_out = jnp.where(lane == r, idx, idx_out)
        logits = jnp.where(lane == idx, NEG, logits)
        top.append(m)
    es = [jnp.exp(m - top[0]) for m in top]
    denom = es[0] + es[1] + es[2] + es[3]
    gates = jnp.zeros(logits.shape, F32)
    for r in range(TOP_K):
        gates = jnp.where(lane == r, es[r] / denom, gates)
    idx_ref[...] = idx_out
    gate_ref[...] = gates


def _outproj(parts, x2, w_out, g, b, rwh, rwl, rb):
    t = x2.shape[0]
    tm = TM_OUT
    row = lambda w: pl.BlockSpec((tm, w), lambda i: (i, 0))
    full = lambda a: pl.BlockSpec(a.shape, lambda i: (0,) * a.ndim)
    return pl.pallas_call(
        _outproj_kernel,
        grid=(t // tm,),
        in_specs=[row(256)] * 4 + [row(D_MODEL), full(w_out), full(g), full(b), full(rwh), full(rwl), full(rb)],
        out_specs=[row(D_MODEL), row(LANES), row(LANES)],
        out_shape=[jax.ShapeDtypeStruct((t, D_MODEL), F32), jax.ShapeDtypeStruct((t, LANES), I32),
                   jax.ShapeDtypeStruct((t, LANES), F32)],
        compiler_params=_cparams(("parallel",)),
    )(*parts, x2, w_out, g, b, rwh, rwl, rb)


def _moe_kernel(be_ref, nb_ref, cnt_ref, src_ref, nxt_ref, dst_ref, h_hbm, wgu_ref, bgu_ref, wd_ref, bd_ref,
                y_hbm, xbuf, ybuf, gsem, ssem, *, bm, n_blocks):
    i = pl.program_id(0)
    nb = nb_ref[0]
    slot = i % 2

    def gather_copy(tok, r, s):
        return pltpu.make_async_copy(h_hbm.at[pl.ds(tok, 1)], xbuf.at[s, pl.ds(r, 1)], gsem.at[s])

    def scatter_copy(dst, r, s):
        return pltpu.make_async_copy(ybuf.at[s, pl.ds(r, 1)], y_hbm.at[pl.ds(dst, 1)], ssem.at[s])

    def for_rows(n, fn, unroll=1):
        def body(r, c):
            fn(r)
            return c
        lax.fori_loop(0, n, body, 0, unroll=unroll)

    def scatter_wait(blk):
        for_rows(cnt_ref[blk], lambda r: scatter_copy(0, r, blk % 2).wait())

    @pl.when(i == 0)
    def _():
        for_rows(bm, lambda r: gather_copy(src_ref[0, 0, r], r, 0).start(), unroll=8)

    @pl.when(i + 1 < nb)
    def _():
        for_rows(bm, lambda r: gather_copy(nxt_ref[0, 0, r], r, 1 - slot).start(), unroll=8)

    @pl.when(i < nb)
    def _():
        for_rows(bm, lambda r: gather_copy(0, r, slot).wait(), unroll=8)

        @pl.when(i >= 2)
        def _():
            scatter_wait(i - 2)

        x = xbuf[slot].astype(BF16)
        acc = jnp.zeros((bm, D_MODEL), F32)
        for j in range(D_EXPERT // HC_MOE):
            lo = j * HC_MOE
            gate = (jnp.dot(x, wgu_ref[0, :, lo:lo + HC_MOE], preferred_element_type=F32)
                    + bgu_ref[0, :, lo:lo + HC_MOE])
            up = (jnp.dot(x, wgu_ref[0, :, D_EXPERT + lo:D_EXPERT + lo + HC_MOE], preferred_element_type=F32)
                  + bgu_ref[0, :, D_EXPERT + lo:D_EXPERT + lo + HC_MOE])
            gate = jnp.minimum(gate, SWIGLU_LIMIT)
            up = jnp.clip(up, -SWIGLU_LIMIT, SWIGLU_LIMIT)
            act = (up + 1.0) * gate * (1.0 / (1.0 + jnp.exp(-SWIGLU_ALPHA * gate)))
            acc = acc + jnp.dot(act.astype(BF16), wd_ref[0, lo:lo + HC_MOE, :], preferred_element_type=F32)
        ybuf[slot] = acc + bd_ref[0]
        for_rows(cnt_ref[i], lambda r: scatter_copy(dst_ref[0, 0, r], r, slot).start())

    @pl.when(i == n_blocks - 1)
    def _():
        @pl.when(nb >= 1)
        def _():
            scatter_wait(nb - 1)

        @pl.when(nb >= 2)
        def _():
            scatter_wait(nb - 2)


def _moe(h, block_e, nb, cnt, row_src, row_dst, wgu, bgu, wd, bd):
    t = h.shape[0]
    bm = BM_MOE
    n_blocks = row_src.shape[0]
    idx_spec = lambda f: pl.BlockSpec((1, 1, bm), f, memory_space=pltpu.SMEM)
    grid_spec = pltpu.PrefetchScalarGridSpec(
        num_scalar_prefetch=3,
        grid=(n_blocks,),
        in_specs=[idx_spec(lambda i, *_: (i, 0, 0)),
                  idx_spec(lambda i, *_: (jnp.minimum(i + 1, n_blocks - 1), 0, 0)),
                  idx_spec(lambda i, *_: (i, 0, 0)),
                  pl.BlockSpec(memory_space=pl.ANY),
                  pl.BlockSpec((1, D_MODEL, 2 * D_EXPERT), lambda i, be, *_: (be[i], 0, 0)),
                  pl.BlockSpec((1, 1, 2 * D_EXPERT), lambda i, be, *_: (be[i], 0, 0)),
                  pl.BlockSpec((1, D_EXPERT, D_MODEL), lambda i, be, *_: (be[i], 0, 0)),
                  pl.BlockSpec((1, 1, D_MODEL), lambda i, be, *_: (be[i], 0, 0))],
        out_specs=pl.BlockSpec(memory_space=pl.ANY),
        scratch_shapes=[pltpu.VMEM((2, bm, D_MODEL), F32), pltpu.VMEM((2, bm, D_MODEL), F32),
                        pltpu.SemaphoreType.DMA((2,)), pltpu.SemaphoreType.DMA((2,))])
    return pl.pallas_call(
        functools.partial(_moe_kernel, bm=bm, n_blocks=n_blocks),
        grid_spec=grid_spec,
        out_shape=jax.ShapeDtypeStruct((TOP_K * t, D_MODEL), F32),
        compiler_params=_cparams(("arbitrary",)),
    )(block_e, nb, cnt, row_src, row_src, row_dst, h, wgu, bgu, wd, bd)


def _route(top_idx, bm):
    t = top_idx.shape[0]
    m = t * TOP_K
    flat_e = top_idx.reshape(m)
    onehot = (flat_e[:, None] == jnp.arange(N_EXPERTS, dtype=I32)[None, :]).astype(I32)
    cum = jnp.cumsum(onehot, axis=0)
    counts = cum[-1]
    rank = jnp.sum(onehot * cum, axis=1) - 1
    nblk_e = (counts + bm - 1) // bm
    blk_end = jnp.cumsum(nblk_e)
    blk_start = blk_end - nblk_e
    dest = jnp.sum(onehot * blk_start[None, :], axis=1) * bm + rank
    n_blocks = m // bm + N_EXPERTS
    n_rows = n_blocks * bm
    row_m = jnp.full((n_rows,), -1, I32).at[dest].set(jnp.arange(m, dtype=I32), unique_indices=True)
    row_m = jnp.maximum(row_m, 0)
    blk = jnp.arange(n_blocks, dtype=I32)
    block_e = jnp.minimum(jnp.searchsorted(blk_end, blk, side='right'), N_EXPERTS - 1).astype(I32)
    nb = blk_end[-1:].astype(I32)
    cnt = jnp.clip(counts[block_e] - (blk - blk_start[block_e]) * bm, 0, bm)
    cnt = jnp.where(blk < nb[0], cnt, 0).astype(I32)
    return (block_e, nb, cnt, (row_m // TOP_K).reshape(n_blocks, 1, bm), row_m.reshape(n_blocks, 1, bm))


def _combine_kernel(y_ref, gate_ref, h_ref, g_ref, b_ref, o_ref):
    gates = gate_ref[...]
    f = jnp.zeros(o_ref.shape, F32)
    for k in range(TOP_K):
        f = f + gates[:, k:k + 1] * y_ref[:, k * D_MODEL:(k + 1) * D_MODEL]
    o_ref[...] = _layer_norm(DEEPNORM_ALPHA * h_ref[...] + f, g_ref[...], b_ref[...])


def _combine(y4, gates, h, g, b):
    t = h.shape[0]
    tm = TM_COMB
    row = lambda w: pl.BlockSpec((tm, w), lambda i: (i, 0))
    full = lambda a: pl.BlockSpec(a.shape, lambda i: (0,) * a.ndim)
    return pl.pallas_call(
        _combine_kernel,
        grid=(t // tm,),
        in_specs=[row(TOP_K * D_MODEL), row(LANES), row(D_MODEL), full(g), full(b)],
        out_specs=row(D_MODEL),
        out_shape=jax.ShapeDtypeStruct((t, D_MODEL), F32),
        compiler_params=_cparams(("parallel",)),
    )(y4, gates, h, g, b)


def _pad_cols(w, width):
    return jnp.pad(w, ((0, 0), (0, width - w.shape[1])))


def _pack_layer(w_in, gla_w_gate, gla_b_gate, gla_norm, pool_w, pool_scale, diff_lq1, diff_lk1,
                diff_lq2, diff_lk2, diff_norm, mla_q_norm, mla_w_uq, mla_kv_norm, mla_w_ukv,
                router_w, router_b):
    offs = [0, 128, 256, 512, 768, 784, 1040, 1296, 1552, 1808, 2000, 2160]
    seg = lambda i: w_in[:, offs[i]:offs[i + 1]]
    w_big = jnp.concatenate(
        [seg(0), seg(1), seg(2), seg(3), _pad_cols(seg(4), LANES), seg(5), seg(6), seg(7), seg(8),
         _pad_cols(seg(9), 256), _pad_cols(seg(10), 256)], axis=1).astype(BF16)
    wg = jnp.pad(gla_w_gate, ((0, LANES - GLA_GATE_RANK), (0, 0))).astype(BF16)
    bg = gla_b_gate.reshape(1, LANES)
    gnorm = jnp.tile(gla_norm, N_HEADS).reshape(1, 256)
    eye = jnp.eye(N_HEADS, dtype=F32)
    ind = jnp.kron(eye, jnp.ones((GLA_DK, GLA_DV), F32)).astype(BF16)
    tri = jnp.tril(jnp.ones((GLA_CHUNK, GLA_CHUNK), F32)).astype(BF16)
    pool_bd = jax.scipy.linalg.block_diag(*[pool_w[g] for g in range(4)]).astype(BF16)
    pscale = pool_scale.reshape(1, 256)
    aux = jnp.zeros((8, LANES), F32)
    aux = aux.at[0, :DIFF_DH].set(diff_lq1).at[1, :DIFF_DH].set(diff_lk1)
    aux = aux.at[2, :DIFF_DH].set(diff_lq2).at[3, :DIFF_DH].set(diff_lk2)
    aux = aux.at[4, :2 * DIFF_DH].set(diff_norm)
    qn = _pad_cols(mla_q_norm.reshape(1, MLA_Q_RANK), 256)
    kvn = mla_kv_norm.reshape(1, MLA_KV_RANK)
    dqk = MLA_NOPE + MLA_ROPE
    wuq = jnp.concatenate([_pad_cols(mla_w_uq[:, h * dqk:(h + 1) * dqk], LANES) for h in range(N_HEADS)], axis=1)
    wuq = jnp.pad(wuq, ((0, 256 - MLA_Q_RANK), (0, 0))).astype(BF16)
    dkv = MLA_NOPE + MLA_V
    wk = jnp.concatenate([_pad_cols(mla_w_ukv[:, h * dkv:h * dkv + MLA_NOPE], LANES) for h in range(N_HEADS)], axis=1)
    wv = jnp.concatenate([mla_w_ukv[:, h * dkv + MLA_NOPE:(h + 1) * dkv] for h in range(N_HEADS)], axis=1)
    wukv = jnp.concatenate([wk, wv], axis=1).astype(BF16)
    sel = jnp.zeros((LANES, 4 * LANES), F32)
    for h in range(N_HEADS):
        sel = sel.at[jnp.arange(MLA_ROPE), h * LANES + MLA_NOPE + jnp.arange(MLA_ROPE)].set(1.0)
    sel = sel.astype(BF16)
    rw = _pad_cols(router_w, LANES)
    rwh = rw.astype(BF16)
    rwl = (rw - rwh.astype(F32)).astype(BF16)
    rb = _pad_cols(router_b.reshape(1, N_EXPERTS), LANES)
    return dict(w_big=w_big, wg=wg, bg=bg, gnorm=gnorm, ind=ind, tri=tri, pool_bd=pool_bd, pscale=pscale,
                aux=aux, qn=qn, kvn=kvn, wuq=wuq, wukv=wukv, sel=sel, rwh=rwh, rwl=rwl, rb=rb)


def kernel(x, positions, w_in, gla_w_gate, gla_b_gate, gla_norm, pool_w, pool_scale, diff_lq1, diff_lk1,
           diff_lq2, diff_lk2, diff_norm, mla_q_norm, mla_w_uq, mla_kv_norm, mla_w_ukv, w_out, ln1_g, ln1_b,
           router_w, router_b, w_gate_up, b_gate_up, w_down, b_down, ln2_g, ln2_b):
    batch, seq, d = x.shape
    t = batch * seq
    assert d == D_MODEL and seq % TQ == 0 and t % TM_PROJ == 0 and (t * TOP_K) % BM_MOE == 0

    inv = 1.0 / (ROPE_THETA ** (jnp.arange(0, DIFF_DH, 2, dtype=F32) / DIFF_DH))
    ang = positions.astype(F32).reshape(t, 1) * inv[None, :]
    cos, sin = jnp.cos(ang), jnp.sin(ang)
    cos128 = jnp.tile(jnp.concatenate([cos, cos], axis=1), (1, 4))
    sin128 = jnp.tile(jnp.concatenate([-sin, sin], axis=1), (1, 4))

    h = x.reshape(t, d)
    for l in range(DEPTH):
        p = _pack_layer(w_in[l], gla_w_gate[l], gla_b_gate[l], gla_norm[l], pool_w[l], pool_scale[l],
                        diff_lq1[l], diff_lk1[l], diff_lq2[l], diff_lk2[l], diff_norm[l], mla_q_norm[l],
                        mla_w_uq[l], mla_kv_norm[l], mla_w_ukv[l], router_w[l], router_b[l])
        gla_in, pool_in, dq, dk, dv, mq, mk, mv = _proj(
            h, p['w_big'], cos128, sin128, p['qn'], p['kvn'], p['wuq'], p['wukv'], p['sel'])
        gla_out = _gla(gla_in, p['wg'], p['bg'], p['gnorm'], p['ind'], p['tri'], batch)
        pool_out = _pool(pool_in, p['pool_bd'], p['pscale'], batch)
        lam_init = 0.8 - 0.6 * math.exp(-0.3 * l)
        diff_out = _flash(dq, dk, dv, p['aux'], batch, DIFF_HEADS_SPEC, lam_init)
        mla_out = _flash(mq, mk, mv, p['aux'], batch, MLA_HEADS_SPEC, None)
        h1, top_idx, gates = _outproj(
            (gla_out, pool_out, diff_out, mla_out), h, w_out[l].astype(BF16), ln1_g[l].reshape(1, d),
            ln1_b[l].reshape(1, d), p['rwh'], p['rwl'], p['rb'])
        block_e, nb, cnt, row_src, row_dst = _route(top_idx[:, :TOP_K], BM_MOE)
        ybuf = _moe(h1, block_e, nb, cnt, row_src, row_dst, w_gate_up[l].astype(BF16),
                    b_gate_up[l].reshape(N_EXPERTS, 1, 2 * D_EXPERT), w_down[l].astype(BF16),
                    b_down[l].reshape(N_EXPERTS, 1, D_MODEL))
        y4 = ybuf.reshape(t, TOP_K * D_MODEL)
        h = _combine(y4, gates, h1, ln2_g[l].reshape(1, d), ln2_b[l].reshape(1, d))
    return h.reshape(batch, seq, d)
```

```python
import functools
import math

import jax
import jax.numpy as jnp
from jax import lax
from jax.experimental import pallas as pl
from jax.experimental.pallas import tpu as pltpu

F32 = jnp.float32
BF16 = jnp.bfloat16
I32 = jnp.int32

D_MODEL = 1024
DEPTH = 2
GROUP_WIDTH = 256
N_HEADS = 4
GLA_DK = 32
GLA_DV = 64
GLA_GATE_RANK = 16
GLA_GATE_NORM = 16.0
POOL_WINDOWS = (2, 4, 8, 16)
POOL_CH = 64
DIFF_DH = 32
MLA_Q_RANK = 192
MLA_KV_RANK = 128
MLA_NOPE = 64
MLA_ROPE = 32
MLA_V = 64
ROPE_THETA = 10000.0
N_EXPERTS = 32
TOP_K = 4
D_EXPERT = 1024
SWIGLU_LIMIT = 7.0
SWIGLU_ALPHA = 1.702
NORM_EPS = 1e-5
DEEPNORM_ALPHA = (2 * DEPTH) ** 0.25

LANES = 128
MXU_WIDTH = 256
VMEM_LIMIT = 56 * 1024 * 1024

C_GLA = 0
C_POOL = 896
C_DQ = 1152
C_DK = 1408
C_DV = 1664
C_MQA = 1920
C_MKVA = 2176
C_END = 2432
GLA_W = C_POOL - C_GLA

NEG = -1e30

TM_PROJ = 512
TG_GLA = 512
GLA_CHUNK = 64
TQ = 512
TM_OUT = 512
BM_MOE = 512
HC_MOE = 512
TM_DISP = 128
TM_COMB = 128


def _cparams(sem):
    return pltpu.CompilerParams(dimension_semantics=sem, vmem_limit_bytes=VMEM_LIMIT)


def _lane_iota(shape):
    return lax.broadcasted_iota(I32, shape, len(shape) - 1)


def _row_iota(shape):
    return lax.broadcasted_iota(I32, shape, len(shape) - 2)


def _rope(x, cos, sin_signed):
    w = x.shape[-1]
    from_low = pltpu.roll(x, 16, axis=1)
    from_high = pltpu.roll(x, w - 16, axis=1)
    swapped = jnp.where((_lane_iota(x.shape) % 32) >= 16, from_low, from_high)
    return x * cos + swapped * sin_signed


def _proj_kernel(x_ref, w_ref, cos_ref, sin_ref, qn_ref, kvn_ref, wuq_ref, wukv_ref, sel_ref,
                 gla_ref, pool_ref, dq_ref, dk_ref, dv_ref, mq_ref, mk_ref, mv_ref):
    xb = x_ref[...].astype(BF16)

    def seg(lo, hi):
        return jnp.dot(xb, w_ref[:, lo:hi], preferred_element_type=F32)

    gla_ref[...] = seg(C_GLA, C_POOL)
    pool_ref[...] = seg(C_POOL, C_DQ)

    cos = cos_ref[...]
    sin = sin_ref[...]
    cos2 = jnp.concatenate([cos, cos], axis=1)
    sin2 = jnp.concatenate([sin, sin], axis=1)
    dq_ref[...] = (_rope(seg(C_DQ, C_DK), cos2, sin2) * (DIFF_DH ** -0.5)).astype(BF16)
    dk_ref[...] = _rope(seg(C_DK, C_DV), cos2, sin2).astype(BF16)
    dv_ref[...] = seg(C_DV, C_MQA).astype(BF16)

    mqa = seg(C_MQA, C_MKVA)
    ms = jnp.sum(mqa * mqa, axis=1, keepdims=True) * (1.0 / MLA_Q_RANK)
    cq = (mqa * lax.rsqrt(ms + 1e-6) * qn_ref[...]).astype(BF16)
    mq = jnp.dot(cq, wuq_ref[...], preferred_element_type=F32)
    cos4 = jnp.concatenate([cos2, cos2], axis=1)
    sin4 = jnp.concatenate([sin2, sin2], axis=1)
    lane = _lane_iota(mq.shape) % LANES
    is_rope = (lane >= MLA_NOPE) & (lane < MLA_NOPE + MLA_ROPE)
    mq = jnp.where(is_rope, _rope(mq, cos4, sin4), mq)
    mq_ref[...] = (mq * ((MLA_NOPE + MLA_ROPE) ** -0.5)).astype(BF16)

    mkva = seg(C_MKVA, C_END)
    ckv_raw = mkva[:, :LANES]
    msk = jnp.sum(ckv_raw * ckv_raw, axis=1, keepdims=True) * (1.0 / MLA_KV_RANK)
    ckv = (ckv_raw * lax.rsqrt(msk + 1e-6) * kvn_ref[...]).astype(BF16)
    kv = jnp.dot(ckv, wukv_ref[...], preferred_element_type=F32)
    k_rope = _rope(mkva[:, LANES:], cos, sin).astype(BF16)
    placed = jnp.dot(k_rope, sel_ref[...], preferred_element_type=F32)
    mk_ref[...] = (kv[:, :4 * LANES] + placed).astype(BF16)
    mv_ref[...] = kv[:, 4 * LANES:].astype(BF16)


def _proj(x2, w_big, cos128, sin128, qn, kvn, wuq, wukv, sel):
    t = x2.shape[0]
    tm = TM_PROJ
    row = lambda w: pl.BlockSpec((tm, w), lambda i: (i, 0))
    full = lambda a: pl.BlockSpec(a.shape, lambda i: (0,) * a.ndim)
    out_shapes = [
        jax.ShapeDtypeStruct((t, GLA_W), F32), jax.ShapeDtypeStruct((t, 256), F32),
        jax.ShapeDtypeStruct((t, 256), BF16), jax.ShapeDtypeStruct((t, 256), BF16),
        jax.ShapeDtypeStruct((t, 256), BF16), jax.ShapeDtypeStruct((t, 512), BF16),
        jax.ShapeDtypeStruct((t, 512), BF16), jax.ShapeDtypeStruct((t, 256), BF16)]
    return pl.pallas_call(
        _proj_kernel,
        grid=(t // tm,),
        in_specs=[row(D_MODEL), full(w_big), row(LANES), row(LANES), full(qn), full(kvn),
                  full(wuq), full(wukv), full(sel)],
        out_specs=[row(s.shape[1]) for s in out_shapes],
        out_shape=out_shapes,
        compiler_params=_cparams(("parallel",)),
    )(x2, w_big, cos128, sin128, qn, kvn, wuq, wukv, sel)


def _split3(x):
    hi = x.astype(BF16)
    r1 = x - hi.astype(F32)
    mid = r1.astype(BF16)
    lo = (r1 - mid.astype(F32)).astype(BF16)
    return hi, mid, lo


def _gla_kernel(blk_ref, wg_ref, bg_ref, norm_ref, ind_ref, tri_ref, out_ref, st_ref, *, n_chunks):
    c_len = GLA_CHUNK

    @pl.when(pl.program_id(1) == 0)
    def _():
        st_ref[...] = jnp.zeros_like(st_ref)

    same_head = (_row_iota((256, LANES)) // GLA_DV) == (_lane_iota((256, LANES)) // GLA_DK)
    ind = ind_ref[...]
    tri = tri_ref[...]
    rows = _row_iota((c_len, LANES))

    def chunk(c, carry):
        r0 = pl.multiple_of(c * c_len, c_len)
        blk = blk_ref[pl.ds(r0, c_len), :]
        q = blk[:, 0:128] * (GLA_DK ** -0.5)
        k = blk[:, 128:256]
        v = blk[:, 256:512]
        g = blk[:, 512:768]
        gr = blk[:, 768:896]
        logit = jnp.dot(gr.astype(BF16), wg_ref[...], preferred_element_type=F32) + bg_ref[...]
        log_a = (jnp.minimum(logit, 0.0) - jnp.log(1.0 + jnp.exp(-jnp.abs(logit)))) * (1.0 / GLA_GATE_NORM)
        hi, mid, lo = _split3(log_a)
        b = (jnp.dot(tri, hi, preferred_element_type=F32) + jnp.dot(tri, mid, preferred_element_type=F32)
             + jnp.dot(tri, lo, preferred_element_type=F32))

        o_tiles = [jnp.zeros((8, 256), F32) for _ in range(c_len // 8)]
        for j in range(c_len):
            t0 = j // 8
            i0 = 8 * t0
            e = jnp.exp(b[i0:, :] - b[j:j + 1, :]) * (q[i0:, :] * k[j:j + 1, :])
            e = jnp.where(rows[i0:, :] >= j, e, 0.0)
            p = jnp.dot(e.astype(BF16), ind, preferred_element_type=F32)
            pv = p * v[j:j + 1, :]
            for t in range(t0, c_len // 8):
                o_tiles[t] = o_tiles[t] + pv[8 * (t - t0):8 * (t - t0 + 1), :]
        o = jnp.concatenate(o_tiles, axis=0)

        st = st_ref[...]
        qd = (q * jnp.exp(b)).astype(BF16)
        o = o + lax.dot_general(qd, st.astype(BF16), (((1,), (1,)), ((), ())), preferred_element_type=F32)
        b_last = b[c_len - 1:c_len, :]
        kd = (k * jnp.exp(b_last - b)).astype(BF16)
        upd = lax.dot_general(v.astype(BF16), kd, (((0,), (0,)), ((), ())), preferred_element_type=F32)
        st_ref[...] = st * jnp.exp(b_last) + jnp.where(same_head, upd, 0.0)

        o2 = o * o
        head = _lane_iota(o.shape) // GLA_DV
        scale = jnp.zeros_like(o)
        for h in range(N_HEADS):
            ms = jnp.sum(jnp.where(head == h, o2, 0.0), axis=1, keepdims=True) * (1.0 / GLA_DV)
            scale = jnp.where(head == h, lax.rsqrt(ms + 1e-6), scale)
        silu = g / (1.0 + jnp.exp(-g))
        out_ref[pl.ds(r0, c_len), :] = (o * scale * norm_ref[...] * silu).astype(out_ref.dtype)
        return carry

    lax.fori_loop(0, n_chunks, chunk, 0)


def _gla(gla_in, wg, bg, norm, ind, tri, batch):
    t = gla_in.shape[0]
    s = t // batch
    tg = TG_GLA
    nblk = s // tg
    full = lambda a: pl.BlockSpec(a.shape, lambda b, i: (0,) * a.ndim)
    return pl.pallas_call(
        functools.partial(_gla_kernel, n_chunks=tg // GLA_CHUNK),
        grid=(batch, nblk),
        in_specs=[pl.BlockSpec((tg, GLA_W), lambda b, i: (b * nblk + i, 0)),
                  full(wg), full(bg), full(norm), full(ind), full(tri)],
        out_specs=pl.BlockSpec((tg, 256), lambda b, i: (b * nblk + i, 0)),
        out_shape=jax.ShapeDtypeStruct((t, 256), BF16),
        scratch_shapes=[pltpu.VMEM((256, LANES), F32)],
        compiler_params=_cparams(("parallel", "arbitrary")),
    )(gla_in, wg, bg, norm, ind, tri)


def _pool_kernel(u_ref, w_ref, scale_ref, out_ref):
    u = u_ref[...]
    t = _row_iota(u.shape)

    def shifted(x, k):
        return jnp.where(t >= k, pltpu.roll(x, k, axis=0), 0.0)

    sums = []
    acc = u
    for k in (1, 2, 4, 8):
        acc = acc + shifted(acc, k)
        sums.append(acc)
    group = _lane_iota(u.shape) // POOL_CH
    tf = (t + 1).astype(F32)
    mean = jnp.zeros_like(u)
    for gi, w in enumerate(POOL_WINDOWS):
        mean = jnp.where(group == gi, sums[gi] / jnp.minimum(tf, float(w)), mean)
    pooled = (mean - u).astype(BF16)
    y = jnp.dot(pooled, w_ref[...], preferred_element_type=F32) * scale_ref[...]
    out_ref[...] = y.astype(out_ref.dtype)


def _pool(pool_in, w_bd, scale, batch):
    t = pool_in.shape[0]
    s = t // batch
    full = lambda a: pl.BlockSpec(a.shape, lambda b: (0,) * a.ndim)
    return pl.pallas_call(
        _pool_kernel,
        grid=(batch,),
        in_specs=[pl.BlockSpec((s, 256), lambda b: (b, 0)), full(w_bd), full(scale)],
        out_specs=pl.BlockSpec((s, 256), lambda b: (b, 0)),
        out_shape=jax.ShapeDtypeStruct((t, 256), BF16),
        compiler_params=_cparams(("parallel",)),
    )(pool_in, w_bd, scale)


def _flash_kernel(qi_tab, ki_tab, q_ref, k_ref, v_ref, aux_ref, o_ref, qexp, m_s, l_s, acc_s,
                  *, heads, diff_lam_init):
    step = pl.program_id(1)
    qi = qi_tab[step]
    ki = ki_tab[step]
    tq = q_ref.shape[0]
    tk = k_ref.shape[0]
    dv = GROUP_WIDTH // N_HEADS

    @pl.when(ki == 0)
    def _():
        for vh, (grp, lo, width, _) in enumerate(heads):
            qg = q_ref[:, grp * MXU_WIDTH:(grp + 1) * MXU_WIDTH]
            lane = _lane_iota(qg.shape)
            qexp[vh] = jnp.where((lane >= lo) & (lane < lo + width), qg, jnp.zeros_like(qg))
        m_s[...] = jnp.full_like(m_s, NEG)
        l_s[...] = jnp.zeros_like(l_s)
        acc_s[...] = jnp.zeros_like(acc_s)

    def update(masked):
        if masked:
            keep = _lane_iota((tq, tk)) <= _row_iota((tq, tk))
        for vh, (grp, _, _, hv) in enumerate(heads):
            kg = k_ref[:, grp * MXU_WIDTH:(grp + 1) * MXU_WIDTH]
            s = lax.dot_general(qexp[vh], kg, (((1,), (1,)), ((), ())), preferred_element_type=F32)
            if masked:
                s = jnp.where(keep, s, NEG)
            m_old = m_s[vh]
            m_new = jnp.maximum(m_old, jnp.max(s, axis=1, keepdims=True))
            alpha = jnp.exp(m_old - m_new)
            p = jnp.exp(s - m_new)
            l_s[vh] = alpha * l_s[vh] + jnp.sum(p, axis=1, keepdims=True)
            pv = jnp.dot(p.astype(BF16), v_ref[:, hv * dv:(hv + 1) * dv], preferred_element_type=F32)
            acc_s[vh] = alpha * acc_s[vh] + pv
            m_s[vh] = m_new

    @pl.when(ki < qi)
    def _():
        update(False)

    @pl.when(ki == qi)
    def _():
        update(True)
        outs = []
        if diff_lam_init is None:
            for vh in range(len(heads)):
                outs.append(acc_s[vh] / l_s[vh])
        else:
            aux = aux_ref[...]
            lam = (jnp.exp(jnp.sum(aux[0:1] * aux[1:2], axis=1, keepdims=True))
                   - jnp.exp(jnp.sum(aux[2:3] * aux[3:4], axis=1, keepdims=True)) + diff_lam_init)
            gain = aux[4:5, :dv]
            for h in range(N_HEADS):
                o = acc_s[2 * h] / l_s[2 * h] - lam * (acc_s[2 * h + 1] / l_s[2 * h + 1])
                ms = jnp.mean(o * o, axis=1, keepdims=True)
                outs.append(o * lax.rsqrt(ms + 1e-5) * gain * (1.0 - diff_lam_init))
        o_ref[...] = jnp.concatenate(outs, axis=1).astype(o_ref.dtype)


def _flash(q, k, v, aux, batch, heads, diff_lam_init):
    t, wq = q.shape
    s = t // batch
    tq = TQ
    nq = s // tq
    pairs = [(a, b) for a in range(nq) for b in range(a + 1)]
    qi_tab = jnp.asarray([p[0] for p in pairs], I32)
    ki_tab = jnp.asarray([p[1] for p in pairs], I32)
    nv = len(heads)
    dv = GROUP_WIDTH // N_HEADS
    grid_spec = pltpu.PrefetchScalarGridSpec(
        num_scalar_prefetch=2,
        grid=(batch, len(pairs)),
        in_specs=[pl.BlockSpec((tq, wq), lambda b, i, qt, kt: (b * nq + qt[i], 0)),
                  pl.BlockSpec((tq, wq), lambda b, i, qt, kt: (b * nq + kt[i], 0)),
                  pl.BlockSpec((tq, 256), lambda b, i, qt, kt: (b * nq + kt[i], 0)),
                  pl.BlockSpec(aux.shape, lambda b, i, qt, kt: (0, 0))],
        out_specs=pl.BlockSpec((tq, 256), lambda b, i, qt, kt: (b * nq + qt[i], 0)),
        scratch_shapes=[pltpu.VMEM((nv, tq, MXU_WIDTH), BF16), pltpu.VMEM((nv, tq, 1), F32),
                        pltpu.VMEM((nv, tq, 1), F32), pltpu.VMEM((nv, tq, dv), F32)])
    return pl.pallas_call(
        functools.partial(_flash_kernel, heads=heads, diff_lam_init=diff_lam_init),
        grid_spec=grid_spec,
        out_shape=jax.ShapeDtypeStruct((t, 256), BF16),
        compiler_params=_cparams(("parallel", "arbitrary")),
    )(qi_tab, ki_tab, q, k, v, aux)


DIFF_HEADS_SPEC = tuple((0, 32 * (2 * h + m), 32, h) for h in range(N_HEADS) for m in range(2))
MLA_HEADS_SPEC = tuple((h // 2, LANES * (h % 2), LANES, h) for h in range(N_HEADS))


def _layer_norm(z, g, b):
    mu = jnp.mean(z, axis=1, keepdims=True)
    zc = z - mu
    var = jnp.mean(zc * zc, axis=1, keepdims=True)
    return zc * lax.rsqrt(var + NORM_EPS) * g + b


def _outproj_kernel(a_ref, p_ref, c_ref, d_ref, x_ref, w_ref, g_ref, b_ref, rwh_ref, rwl_ref, rb_ref, tri_ref,
                    h_ref, idx_ref, gate_ref, rank_ref, cnt_ref):
    y = jnp.zeros(h_ref.shape, F32)
    for i, part in enumerate((a_ref, p_ref, c_ref, d_ref)):
        y = y + jnp.dot(part[...], w_ref[i * GROUP_WIDTH:(i + 1) * GROUP_WIDTH, :],
                        preferred_element_type=F32)
    h = _layer_norm(DEEPNORM_ALPHA * x_ref[...] + y, g_ref[...], b_ref[...])
    h_ref[...] = h

    hi = h.astype(BF16)
    lo = (h - hi.astype(F32)).astype(BF16)
    logits = (jnp.dot(hi, rwh_ref[...], preferred_element_type=F32)
              + jnp.dot(lo, rwh_ref[...], preferred_element_type=F32)
              + jnp.dot(hi, rwl_ref[...], preferred_element_type=F32) + rb_ref[...])
    lane = _lane_iota(logits.shape)
    logits = jnp.where(lane < N_EXPERTS, logits, NEG)
    idx_out = jnp.zeros(logits.shape, I32)
    chosen = jnp.zeros(logits.shape, F32)
    top, picks = [], []
    for r in range(TOP_K):
        m = jnp.max(logits, axis=1, keepdims=True)
        idx = jnp.min(jnp.where(logits == m, lane, LANES), axis=1, keepdims=True)
        idx_out = jnp.where(lane == r, idx, idx_out)
        pick = lane == idx
        chosen = jnp.where(pick, 1.0, chosen)
        logits = jnp.where(pick, NEG, logits)
        top.append(m)
        picks.append(pick)
    es = [jnp.exp(m - top[0]) for m in top]
    denom = es[0] + es[1] + es[2] + es[3]
    gates = jnp.zeros(logits.shape, F32)
    for r in range(TOP_K):
        gates = jnp.where(lane == r, es[r] / denom, gates)
    idx_ref[...] = idx_out
    gate_ref[...] = gates

    seen = jnp.dot(tri_ref[...], chosen.astype(BF16), preferred_element_type=F32)
    rank = jnp.zeros(logits.shape, F32)
    for r in range(TOP_K):
        before = jnp.sum(jnp.where(picks[r], seen, 0.0), axis=1, keepdims=True) - 1.0
        rank = jnp.where(lane == r, before, rank)
    rank_ref[...] = rank.astype(I32)
    cnt_ref[0] = seen[seen.shape[0] - 8:, :].astype(I32)


def _outproj(parts, x2, w_out, g, b, rwh, rwl, rb, tri):
    t = x2.shape[0]
    tm = TM_OUT
    row = lambda w: pl.BlockSpec((tm, w), lambda i: (i, 0))
    full = lambda a: pl.BlockSpec(a.shape, lambda i: (0,) * a.ndim)
    return pl.pallas_call(
        _outproj_kernel,
        grid=(t // tm,),
        in_specs=[row(256)] * 4 + [row(D_MODEL), full(w_out), full(g), full(b), full(rwh), full(rwl), full(rb),
                                   full(tri)],
        out_specs=[row(D_MODEL), row(LANES), row(LANES), row(LANES),
                   pl.BlockSpec((1, 8, LANES), lambda i: (i, 0, 0))],
        out_shape=[jax.ShapeDtypeStruct((t, D_MODEL), F32), jax.ShapeDtypeStruct((t, LANES), I32),
                   jax.ShapeDtypeStruct((t, LANES), F32), jax.ShapeDtypeStruct((t, LANES), I32),
                   jax.ShapeDtypeStruct((t // tm, 8, LANES), I32)],
        compiler_params=_cparams(("parallel",)),
    )(*parts, x2, w_out, g, b, rwh, rwl, rb, tri)


def _route(top_idx, rank, tile_cnt, bm, tm):
    t = top_idx.shape[0]
    n_tiles = t // tm
    tile_cnt = tile_cnt[:, :N_EXPERTS]
    tile_base = jnp.cumsum(tile_cnt, axis=0) - tile_cnt
    counts = jnp.sum(tile_cnt, axis=0)
    nblk_e = (counts + bm - 1) // bm
    blk_end = jnp.cumsum(nblk_e)
    blk_start = blk_end - nblk_e
    offs = blk_start[None, :] * bm + tile_base
    onehot = top_idx.reshape(n_tiles, tm, TOP_K, 1) == jnp.arange(N_EXPERTS, dtype=I32)
    dest = rank.reshape(n_tiles, tm, TOP_K) + jnp.sum(jnp.where(onehot, offs[:, None, None, :], 0), axis=-1)
    n_blocks = (t * TOP_K) // bm + N_EXPERTS
    blk = jnp.arange(n_blocks, dtype=I32)
    block_e = jnp.minimum(jnp.sum(blk[:, None] >= blk_end[None, :], axis=1), N_EXPERTS - 1).astype(I32)
    nb = blk_end[-1:].astype(I32)
    cnt = jnp.clip(counts[block_e] - (blk - blk_start[block_e]) * bm, 0, bm)
    cnt = jnp.where(blk < nb[0], cnt, 0).astype(I32)
    return dest.reshape(t * TOP_K).astype(I32), block_e, nb, cnt


def _dispatch_kernel(cnt_ref, dest_ref, h_hbm, xs_hbm, zbuf, sem, *, tm, n_steps, bm, n_blocks):
    i = pl.program_id(0)
    slot = i % 2

    @pl.when(i == 0)
    def _():
        zbuf[...] = jnp.zeros_like(zbuf)

        def zero_blocks(act):
            def body(blk, c):
                @pl.when(cnt_ref[blk] < bm)
                def _():
                    act(pltpu.make_async_copy(zbuf, xs_hbm.at[pl.ds(pl.multiple_of(blk * bm, bm), bm)], sem.at[2]))
                return c
            lax.fori_loop(0, n_blocks, body, 0)

        zero_blocks(lambda cp: cp.start())
        zero_blocks(lambda cp: cp.wait())

    def copy(tok, dst, s):
        return pltpu.make_async_copy(h_hbm.at[pl.ds(tok, 1)], xs_hbm.at[pl.ds(dst, 1)], sem.at[s])

    for r in range(tm):
        for k in range(TOP_K):
            copy(i * tm + r, dest_ref[0, 0, r * TOP_K + k], slot).start()

    def drain(s):
        for _ in range(tm * TOP_K):
            copy(0, 0, s).wait()

    @pl.when(i >= 1)
    def _():
        drain(1 - slot)

    @pl.when(i == n_steps - 1)
    def _():
        drain(slot)


def _dispatch(h, dest, cnt, n_rows):
    t = h.shape[0]
    tm = TM_DISP
    bm = BM_MOE
    n_steps = t // tm
    grid_spec = pltpu.PrefetchScalarGridSpec(
        num_scalar_prefetch=1,
        grid=(n_steps,),
        in_specs=[pl.BlockSpec((1, 1, tm * TOP_K), lambda i, *_: (i, 0, 0), memory_space=pltpu.SMEM),
                  pl.BlockSpec(memory_space=pl.ANY)],
        out_specs=pl.BlockSpec(memory_space=pl.ANY),
        scratch_shapes=[pltpu.VMEM((bm, D_MODEL), F32), pltpu.SemaphoreType.DMA((3,))])
    return pl.pallas_call(
        functools.partial(_dispatch_kernel, tm=tm, n_steps=n_steps, bm=bm, n_blocks=n_rows // bm),
        grid_spec=grid_spec,
        out_shape=jax.ShapeDtypeStruct((n_rows, D_MODEL), F32),
        compiler_params=_cparams(("arbitrary",)),
    )(cnt, dest.reshape(n_steps, 1, tm * TOP_K), h)


def _expert_kernel(be_ref, nb_ref, x_ref, wgu_ref, bgu_ref, wd_ref, bd_ref, y_ref, wgu_bf, wd_bf):
    i = pl.program_id(0)
    live = i < nb_ref[0]
    first_of_expert = (i == 0) | (be_ref[i] != be_ref[jnp.maximum(i - 1, 0)])

    @pl.when(live & first_of_expert)
    def _():
        wgu_bf[...] = wgu_ref[0].astype(BF16)
        wd_bf[...] = wd_ref[0].astype(BF16)

    @pl.when(live)
    def _():
        x = x_ref[...].astype(BF16)
        acc = jnp.zeros(y_ref.shape, F32)
        for j in range(D_EXPERT // HC_MOE):
            lo = j * HC_MOE
            gate = (jnp.dot(x, wgu_bf[:, lo:lo + HC_MOE], preferred_element_type=F32)
                    + bgu_ref[0, :, lo:lo + HC_MOE])
            up = (jnp.dot(x, wgu_bf[:, D_EXPERT + lo:D_EXPERT + lo + HC_MOE], preferred_element_type=F32)
                  + bgu_ref[0, :, D_EXPERT + lo:D_EXPERT + lo + HC_MOE])
            gate = jnp.minimum(gate, SWIGLU_LIMIT)
            up = jnp.clip(up, -SWIGLU_LIMIT, SWIGLU_LIMIT)
            act = (up + 1.0) * gate * (1.0 / (1.0 + jnp.exp(-SWIGLU_ALPHA * gate)))
            acc = acc + jnp.dot(act.astype(BF16), wd_bf[lo:lo + HC_MOE, :], preferred_element_type=F32)
        y_ref[...] = acc + bd_ref[0]

    @pl.when(jnp.logical_not(live))
    def _():
        y_ref[...] = jnp.zeros_like(y_ref)


def _experts(xs, block_e, nb, wgu, bgu, wd, bd):
    bm = BM_MOE
    n_blocks = xs.shape[0] // bm
    grid_spec = pltpu.PrefetchScalarGridSpec(
        num_scalar_prefetch=2,
        grid=(n_blocks,),
        in_specs=[pl.BlockSpec((bm, D_MODEL), lambda i, *_: (i, 0)),
                  pl.BlockSpec((1, D_MODEL, 2 * D_EXPERT), lambda i, be, *_: (be[i], 0, 0)),
                  pl.BlockSpec((1, 1, 2 * D_EXPERT), lambda i, be, *_: (be[i], 0, 0)),
                  pl.BlockSpec((1, D_EXPERT, D_MODEL), lambda i, be, *_: (be[i], 0, 0)),
                  pl.BlockSpec((1, 1, D_MODEL), lambda i, be, *_: (be[i], 0, 0))],
        out_specs=pl.BlockSpec((bm, D_MODEL), lambda i, *_: (i, 0)),
        scratch_shapes=[pltpu.VMEM((D_MODEL, 2 * D_EXPERT), BF16), pltpu.VMEM((D_EXPERT, D_MODEL), BF16)])
    return pl.pallas_call(
        _expert_kernel,
        grid_spec=grid_spec,
        out_shape=jax.ShapeDtypeStruct(xs.shape, F32),
        compiler_params=_cparams(("arbitrary",)),
    )(block_e, nb, xs, wgu, bgu, wd, bd)


def _combine_kernel(dest_ref, nxt_ref, gate_ref, h_ref, g_ref, b_ref, ys_hbm, o_ref, buf, sem, *, tm, n_steps):
    i = pl.program_id(0)
    slot = i % 2

    def copy(src, r, k, s):
        return pltpu.make_async_copy(ys_hbm.at[pl.ds(src, 1)], buf.at[s, k, pl.ds(r, 1)], sem.at[s])

    def issue(idx_ref, s):
        for r in range(tm):
            for k in range(TOP_K):
                copy(idx_ref[0, 0, r * TOP_K + k], r, k, s).start()

    @pl.when(i == 0)
    def _():
        issue(dest_ref, 0)

    @pl.when(i + 1 < n_steps)
    def _():
        issue(nxt_ref, 1 - slot)

    for _ in range(tm * TOP_K):
        copy(0, 0, 0, slot).wait()

    gates = gate_ref[...]
    f = jnp.zeros(o_ref.shape, F32)
    for k in range(TOP_K):
        f = f + gates[:, k:k + 1] * buf[slot, k]
    o_ref[...] = _layer_norm(DEEPNORM_ALPHA * h_ref[...] + f, g_ref[...], b_ref[...])


def _combine(ys, dest, gates, h, g, b):
    t = h.shape[0]
    tm = TM_COMB
    n_steps = t // tm
    row = lambda w: pl.BlockSpec((tm, w), lambda i: (i, 0))
    full = lambda a: pl.BlockSpec(a.shape, lambda i: (0,) * a.ndim)
    idx_spec = lambda f: pl.BlockSpec((1, 1, tm * TOP_K), f, memory_space=pltpu.SMEM)
    dest3 = dest.reshape(n_steps, 1, tm * TOP_K)
    return pl.pallas_call(
        functools.partial(_combine_kernel, tm=tm, n_steps=n_steps),
        grid=(n_steps,),
        in_specs=[idx_spec(lambda i: (i, 0, 0)), idx_spec(lambda i: (jnp.minimum(i + 1, n_steps - 1), 0, 0)),
                  row(LANES), row(D_MODEL), full(g), full(b), pl.BlockSpec(memory_space=pl.ANY)],
        out_specs=row(D_MODEL),
        out_shape=jax.ShapeDtypeStruct((t, D_MODEL), F32),
        scratch_shapes=[pltpu.VMEM((2, TOP_K, tm, D_MODEL), F32), pltpu.SemaphoreType.DMA((2,))],
        compiler_params=_cparams(("arbitrary",)),
    )(dest3, dest3, gates, h, g, b, ys)


def _pad_cols(w, width):
    return jnp.pad(w, ((0, 0), (0, width - w.shape[1])))


def _pack_layer(w_in, gla_w_gate, gla_b_gate, gla_norm, pool_w, pool_scale, diff_lq1, diff_lk1,
                diff_lq2, diff_lk2, diff_norm, mla_q_norm, mla_w_uq, mla_kv_norm, mla_w_ukv,
                router_w, router_b):
    offs = [0, 128, 256, 512, 768, 784, 1040, 1296, 1552, 1808, 2000, 2160]
    seg = lambda i: w_in[:, offs[i]:offs[i + 1]]
    w_big = jnp.concatenate(
        [seg(0), seg(1), seg(2), seg(3), _pad_cols(seg(4), LANES), seg(5), seg(6), seg(7), seg(8),
         _pad_cols(seg(9), 256), _pad_cols(seg(10), 256)], axis=1).astype(BF16)
    wg = jnp.pad(gla_w_gate, ((0, LANES - GLA_GATE_RANK), (0, 0))).astype(BF16)
    bg = gla_b_gate.reshape(1, LANES)
    gnorm = jnp.tile(gla_norm, N_HEADS).reshape(1, 256)
    eye = jnp.eye(N_HEADS, dtype=F32)
    ind = jnp.kron(eye, jnp.ones((GLA_DK, GLA_DV), F32)).astype(BF16)
    tri = jnp.tril(jnp.ones((GLA_CHUNK, GLA_CHUNK), F32)).astype(BF16)
    pool_bd = jax.scipy.linalg.block_diag(*[pool_w[g] for g in range(4)]).astype(BF16)
    pscale = pool_scale.reshape(1, 256)
    aux = jnp.zeros((8, LANES), F32)
    aux = aux.at[0, :DIFF_DH].set(diff_lq1).at[1, :DIFF_DH].set(diff_lk1)
    aux = aux.at[2, :DIFF_DH].set(diff_lq2).at[3, :DIFF_DH].set(diff_lk2)
    aux = aux.at[4, :2 * DIFF_DH].set(diff_norm)
    qn = _pad_cols(mla_q_norm.reshape(1, MLA_Q_RANK), 256)
    kvn = mla_kv_norm.reshape(1, MLA_KV_RANK)
    dqk = MLA_NOPE + MLA_ROPE
    wuq = jnp.concatenate([_pad_cols(mla_w_uq[:, h * dqk:(h + 1) * dqk], LANES) for h in range(N_HEADS)], axis=1)
    wuq = jnp.pad(wuq, ((0, 256 - MLA_Q_RANK), (0, 0))).astype(BF16)
    dkv = MLA_NOPE + MLA_V
    wk = jnp.concatenate([_pad_cols(mla_w_ukv[:, h * dkv:h * dkv + MLA_NOPE], LANES) for h in range(N_HEADS)], axis=1)
    wv = jnp.concatenate([mla_w_ukv[:, h * dkv + MLA_NOPE:(h + 1) * dkv] for h in range(N_HEADS)], axis=1)
    wukv = jnp.concatenate([wk, wv], axis=1).astype(BF16)
    sel = jnp.zeros((LANES, 4 * LANES), F32)
    for h in range(N_HEADS):
        sel = sel.at[jnp.arange(MLA_ROPE), h * LANES + MLA_NOPE + jnp.arange(MLA_ROPE)].set(1.0)
    sel = sel.astype(BF16)
    rw = _pad_cols(router_w, LANES)
    rwh = rw.astype(BF16)
    rwl = (rw - rwh.astype(F32)).astype(BF16)
    rb = _pad_cols(router_b.reshape(1, N_EXPERTS), LANES)
    return dict(w_big=w_big, wg=wg, bg=bg, gnorm=gnorm, ind=ind, tri=tri, pool_bd=pool_bd, pscale=pscale,
                aux=aux, qn=qn, kvn=kvn, wuq=wuq, wukv=wukv, sel=sel, rwh=rwh, rwl=rwl, rb=rb)


def kernel(x, positions, w_in, gla_w_gate, gla_b_gate, gla_norm, pool_w, pool_scale, diff_lq1, diff_lk1,
           diff_lq2, diff_lk2, diff_norm, mla_q_norm, mla_w_uq, mla_kv_norm, mla_w_ukv, w_out, ln1_g, ln1_b,
           router_w, router_b, w_gate_up, b_gate_up, w_down, b_down, ln2_g, ln2_b):
    batch, seq, d = x.shape
    t = batch * seq
    assert d == D_MODEL and seq % TQ == 0 and t % TM_PROJ == 0 and (t * TOP_K) % BM_MOE == 0

    inv = 1.0 / (ROPE_THETA ** (jnp.arange(0, DIFF_DH, 2, dtype=F32) / DIFF_DH))
    ang = positions.astype(F32).reshape(t, 1) * inv[None, :]
    cos, sin = jnp.cos(ang), jnp.sin(ang)
    cos128 = jnp.tile(jnp.concatenate([cos, cos], axis=1), (1, 4))
    sin128 = jnp.tile(jnp.concatenate([-sin, sin], axis=1), (1, 4))

    tri_out = jnp.tril(jnp.ones((TM_OUT, TM_OUT), F32)).astype(BF16)

    h = x.reshape(t, d)
    for l in range(DEPTH):
        p = _pack_layer(w_in[l], gla_w_gate[l], gla_b_gate[l], gla_norm[l], pool_w[l], pool_scale[l],
                        diff_lq1[l], diff_lk1[l], diff_lq2[l], diff_lk2[l], diff_norm[l], mla_q_norm[l],
                        mla_w_uq[l], mla_kv_norm[l], mla_w_ukv[l], router_w[l], router_b[l])
        gla_in, pool_in, dq, dk, dv, mq, mk, mv = _proj(
            h, p['w_big'], cos128, sin128, p['qn'], p['kvn'], p['wuq'], p['wukv'], p['sel'])
        gla_out = _gla(gla_in, p['wg'], p['bg'], p['gnorm'], p['ind'], p['tri'], batch)
        pool_out = _pool(pool_in, p['pool_bd'], p['pscale'], batch)
        lam_init = 0.8 - 0.6 * math.exp(-0.3 * l)
        diff_out = _flash(dq, dk, dv, p['aux'], batch, DIFF_HEADS_SPEC, lam_init)
        mla_out = _flash(mq, mk, mv, p['aux'], batch, MLA_HEADS_SPEC, None)
        h1, top_idx, gates, rank, tile_cnt = _outproj(
            (gla_out, pool_out, diff_out, mla_out), h, w_out[l].astype(BF16), ln1_g[l].reshape(1, d),
            ln1_b[l].reshape(1, d), p['rwh'], p['rwl'], p['rb'], tri_out)
        dest, block_e, nb, cnt = _route(top_idx[:, :TOP_K], rank[:, :TOP_K], tile_cnt[:, 7, :], BM_MOE, TM_OUT)
        n_rows = t * TOP_K + N_EXPERTS * BM_MOE
        xs = _dispatch(h1, dest, cnt, n_rows)
        ys = _experts(xs, block_e, nb, w_gate_up[l], b_gate_up[l].reshape(N_EXPERTS, 1, 2 * D_EXPERT),
                      w_down[l], b_down[l].reshape(N_EXPERTS, 1, D_MODEL))
        h = _combine(ys, dest, gates, h1, ln2_g[l].reshape(1, d), ln2_b[l].reshape(1, d))
    return h.reshape(batch, seq, d)
```

```python
import functools
import math

import jax
import jax.numpy as jnp
from jax import lax
from jax.experimental import pallas as pl
from jax.experimental.pallas import tpu as pltpu

F32 = jnp.float32
BF16 = jnp.bfloat16
I32 = jnp.int32

D_MODEL = 1024
DEPTH = 2
GROUP_WIDTH = 256
N_HEADS = 4
GLA_DK = 32
GLA_DV = 64
GLA_GATE_RANK = 16
GLA_GATE_NORM = 16.0
POOL_WINDOWS = (2, 4, 8, 16)
POOL_CH = 64
DIFF_DH = 32
MLA_Q_RANK = 192
MLA_KV_RANK = 128
MLA_NOPE = 64
MLA_ROPE = 32
MLA_V = 64
ROPE_THETA = 10000.0
N_EXPERTS = 32
TOP_K = 4
D_EXPERT = 1024
SWIGLU_LIMIT = 7.0
SWIGLU_ALPHA = 1.702
NORM_EPS = 1e-5
DEEPNORM_ALPHA = (2 * DEPTH) ** 0.25

LANES = 128
MXU_WIDTH = 256
VMEM_LIMIT = 56 * 1024 * 1024

C_GLA = 0
C_POOL = 896
C_DQ = 1152
C_DK = 1408
C_DV = 1664
C_MQA = 2176
C_MKVA = 2432
C_END = 2688
GLA_W = C_POOL - C_GLA
V_EXT = 4 * LANES

NEG = -1e30

TM_PROJ = 512
TG_GLA = 512
GLA_CHUNK = 64
TQ = 512
TM_OUT = 512
BM_MOE = 512
HC_MOE = 512
TM_DISP = 128
TM_COMB = 128


def _cparams(sem):
    return pltpu.CompilerParams(dimension_semantics=sem, vmem_limit_bytes=VMEM_LIMIT)


def _lane_iota(shape):
    return lax.broadcasted_iota(I32, shape, len(shape) - 1)


def _row_iota(shape):
    return lax.broadcasted_iota(I32, shape, len(shape) - 2)


def _rope(x, cos, sin_signed):
    w = x.shape[-1]
    from_low = pltpu.roll(x, 16, axis=1)
    from_high = pltpu.roll(x, w - 16, axis=1)
    swapped = jnp.where((_lane_iota(x.shape) % 32) >= 16, from_low, from_high)
    return x * cos + swapped * sin_signed


def _proj_kernel(x_ref, w_ref, cos_ref, sin_ref, qn_ref, kvn_ref, wuq_ref, wukv_ref, sel_ref,
                 gla_ref, pool_ref, dq_ref, dk_ref, dv_ref, mq_ref, mk_ref, mv_ref):
    xb = x_ref[...].astype(BF16)

    def seg(lo, hi):
        return jnp.dot(xb, w_ref[:, lo:hi], preferred_element_type=F32)

    gla_ref[...] = seg(C_GLA, C_POOL)
    pool_ref[...] = seg(C_POOL, C_DQ)

    cos = cos_ref[...]
    sin = sin_ref[...]
    cos2 = jnp.concatenate([cos, cos], axis=1)
    sin2 = jnp.concatenate([sin, sin], axis=1)
    dq_ref[...] = (_rope(seg(C_DQ, C_DK), cos2, sin2) * (DIFF_DH ** -0.5)).astype(BF16)
    dk_ref[...] = _rope(seg(C_DK, C_DV), cos2, sin2).astype(BF16)
    ones_cols = ((_lane_iota((1, V_EXT)) % LANES) >= GROUP_WIDTH // N_HEADS).astype(F32)
    dv_ref[...] = (seg(C_DV, C_MQA) + ones_cols).astype(BF16)

    mqa = seg(C_MQA, C_MKVA)
    ms = jnp.sum(mqa * mqa, axis=1, keepdims=True) * (1.0 / MLA_Q_RANK)
    cq = (mqa * lax.rsqrt(ms + 1e-6) * qn_ref[...]).astype(BF16)
    mq = jnp.dot(cq, wuq_ref[...], preferred_element_type=F32)
    cos4 = jnp.concatenate([cos2, cos2], axis=1)
    sin4 = jnp.concatenate([sin2, sin2], axis=1)
    lane = _lane_iota(mq.shape) % LANES
    is_rope = (lane >= MLA_NOPE) & (lane < MLA_NOPE + MLA_ROPE)
    mq = jnp.where(is_rope, _rope(mq, cos4, sin4), mq)
    mq_ref[...] = (mq * ((MLA_NOPE + MLA_ROPE) ** -0.5)).astype(BF16)

    mkva = seg(C_MKVA, C_END)
    ckv_raw = mkva[:, :LANES]
    msk = jnp.sum(ckv_raw * ckv_raw, axis=1, keepdims=True) * (1.0 / MLA_KV_RANK)
    ckv = (ckv_raw * lax.rsqrt(msk + 1e-6) * kvn_ref[...]).astype(BF16)
    kv = jnp.dot(ckv, wukv_ref[...], preferred_element_type=F32)
    k_rope = _rope(mkva[:, LANES:], cos, sin).astype(BF16)
    placed = jnp.dot(k_rope, sel_ref[...], preferred_element_type=F32)
    mk_ref[...] = (kv[:, :4 * LANES] + placed).astype(BF16)
    mv_ref[...] = (kv[:, 4 * LANES:] + ones_cols).astype(BF16)


def _proj(x2, w_big, cos128, sin128, qn, kvn, wuq, wukv, sel):
    t = x2.shape[0]
    tm = TM_PROJ
    row = lambda w: pl.BlockSpec((tm, w), lambda i: (i, 0))
    full = lambda a: pl.BlockSpec(a.shape, lambda i: (0,) * a.ndim)
    out_shapes = [
        jax.ShapeDtypeStruct((t, GLA_W), F32), jax.ShapeDtypeStruct((t, 256), F32),
        jax.ShapeDtypeStruct((t, 256), BF16), jax.ShapeDtypeStruct((t, 256), BF16),
        jax.ShapeDtypeStruct((t, V_EXT), BF16), jax.ShapeDtypeStruct((t, 512), BF16),
        jax.ShapeDtypeStruct((t, 512), BF16), jax.ShapeDtypeStruct((t, V_EXT), BF16)]
    return pl.pallas_call(
        _proj_kernel,
        grid=(t // tm,),
        in_specs=[row(D_MODEL), full(w_big), row(LANES), row(LANES), full(qn), full(kvn),
                  full(wuq), full(wukv), full(sel)],
        out_specs=[row(s.shape[1]) for s in out_shapes],
        out_shape=out_shapes,
        compiler_params=_cparams(("parallel",)),
    )(x2, w_big, cos128, sin128, qn, kvn, wuq, wukv, sel)


def _split3(x):
    hi = x.astype(BF16)
    r1 = x - hi.astype(F32)
    mid = r1.astype(BF16)
    lo = (r1 - mid.astype(F32)).astype(BF16)
    return hi, mid, lo


def _gla_kernel(blk_ref, wg_ref, bg_ref, norm_ref, ind_ref, tri_ref, out_ref, st_ref, *, n_chunks):
    c_len = GLA_CHUNK

    @pl.when(pl.program_id(1) == 0)
    def _():
        st_ref[...] = jnp.zeros_like(st_ref)

    same_head = (_row_iota((256, LANES)) // GLA_DV) == (_lane_iota((256, LANES)) // GLA_DK)
    ind = ind_ref[...]
    tri = tri_ref[...]
    rows = _row_iota((c_len, LANES))

    def chunk(c, carry):
        r0 = pl.multiple_of(c * c_len, c_len)
        blk = blk_ref[pl.ds(r0, c_len), :]
        q = blk[:, 0:128] * (GLA_DK ** -0.5)
        k = blk[:, 128:256]
        v = blk[:, 256:512]
        g = blk[:, 512:768]
        gr = blk[:, 768:896]
        logit = jnp.dot(gr.astype(BF16), wg_ref[...], preferred_element_type=F32) + bg_ref[...]
        log_a = (jnp.minimum(logit, 0.0) - jnp.log(1.0 + jnp.exp(-jnp.abs(logit)))) * (1.0 / GLA_GATE_NORM)
        hi, mid, lo = _split3(log_a)
        b = (jnp.dot(tri, hi, preferred_element_type=F32) + jnp.dot(tri, mid, preferred_element_type=F32)
             + jnp.dot(tri, lo, preferred_element_type=F32))

        o_tiles = [jnp.zeros((8, 256), F32) for _ in range(c_len // 8)]
        for j in range(c_len):
            t0 = j // 8
            i0 = 8 * t0
            e = jnp.exp(b[i0:, :] - b[j:j + 1, :]) * (q[i0:, :] * k[j:j + 1, :])
            e = jnp.where(rows[i0:, :] >= j, e, 0.0)
            p = jnp.dot(e.astype(BF16), ind, preferred_element_type=F32)
            pv = p * v[j:j + 1, :]
            for t in range(t0, c_len // 8):
                o_tiles[t] = o_tiles[t] + pv[8 * (t - t0):8 * (t - t0 + 1), :]
        o = jnp.concatenate(o_tiles, axis=0)

        st = st_ref[...]
        qd = (q * jnp.exp(b)).astype(BF16)
        o = o + lax.dot_general(qd, st.astype(BF16), (((1,), (1,)), ((), ())), preferred_element_type=F32)
        b_last = b[c_len - 1:c_len, :]
        kd = (k * jnp.exp(b_last - b)).astype(BF16)
        upd = lax.dot_general(v.astype(BF16), kd, (((0,), (0,)), ((), ())), preferred_element_type=F32)
        st_ref[...] = st * jnp.exp(b_last) + jnp.where(same_head, upd, 0.0)

        o2 = o * o
        head = _lane_iota(o.shape) // GLA_DV
        scale = jnp.zeros_like(o)
        for h in range(N_HEADS):
            ms = jnp.sum(jnp.where(head == h, o2, 0.0), axis=1, keepdims=True) * (1.0 / GLA_DV)
            scale = jnp.where(head == h, lax.rsqrt(ms + 1e-6), scale)
        silu = g / (1.0 + jnp.exp(-g))
        out_ref[pl.ds(r0, c_len), :] = (o * scale * norm_ref[...] * silu).astype(out_ref.dtype)
        return carry

    lax.fori_loop(0, n_chunks, chunk, 0)


def _gla(gla_in, wg, bg, norm, ind, tri, batch):
    t = gla_in.shape[0]
    s = t // batch
    tg = TG_GLA
    nblk = s // tg
    full = lambda a: pl.BlockSpec(a.shape, lambda b, i: (0,) * a.ndim)
    return pl.pallas_call(
        functools.partial(_gla_kernel, n_chunks=tg // GLA_CHUNK),
        grid=(batch, nblk),
        in_specs=[pl.BlockSpec((tg, GLA_W), lambda b, i: (b * nblk + i, 0)),
                  full(wg), full(bg), full(norm), full(ind), full(tri)],
        out_specs=pl.BlockSpec((tg, 256), lambda b, i: (b * nblk + i, 0)),
        out_shape=jax.ShapeDtypeStruct((t, 256), BF16),
        scratch_shapes=[pltpu.VMEM((256, LANES), F32)],
        compiler_params=_cparams(("parallel", "arbitrary")),
    )(gla_in, wg, bg, norm, ind, tri)


def _pool_kernel(u_ref, w_ref, scale_ref, out_ref):
    u = u_ref[...]
    t = _row_iota(u.shape)

    def shifted(x, k):
        return jnp.where(t >= k, pltpu.roll(x, k, axis=0), 0.0)

    sums = []
    acc = u
    for k in (1, 2, 4, 8):
        acc = acc + shifted(acc, k)
        sums.append(acc)
    group = _lane_iota(u.shape) // POOL_CH
    tf = (t + 1).astype(F32)
    mean = jnp.zeros_like(u)
    for gi, w in enumerate(POOL_WINDOWS):
        mean = jnp.where(group == gi, sums[gi] / jnp.minimum(tf, float(w)), mean)
    pooled = (mean - u).astype(BF16)
    y = jnp.dot(pooled, w_ref[...], preferred_element_type=F32) * scale_ref[...]
    out_ref[...] = y.astype(out_ref.dtype)


def _pool(pool_in, w_bd, scale, batch):
    t = pool_in.shape[0]
    s = t // batch
    full = lambda a: pl.BlockSpec(a.shape, lambda b: (0,) * a.ndim)
    return pl.pallas_call(
        _pool_kernel,
        grid=(batch,),
        in_specs=[pl.BlockSpec((s, 256), lambda b: (b, 0)), full(w_bd), full(scale)],
        out_specs=pl.BlockSpec((s, 256), lambda b: (b, 0)),
        out_shape=jax.ShapeDtypeStruct((t, 256), BF16),
        compiler_params=_cparams(("parallel",)),
    )(pool_in, w_bd, scale)


def _flash_kernel(qi_tab, ki_tab, q_ref, k_ref, v_ref, aux_ref, o_ref, qexp, m_s, acc_s,
                  *, heads, diff_lam_init):
    step = pl.program_id(1)
    qi = qi_tab[step]
    ki = ki_tab[step]
    tq = q_ref.shape[0]
    tk = k_ref.shape[0]
    dv = GROUP_WIDTH // N_HEADS

    @pl.when(ki == 0)
    def _():
        for vh, (grp, lo, width, _) in enumerate(heads):
            qg = q_ref[:, grp * LANES:(grp + 1) * LANES]
            lane = _lane_iota(qg.shape)
            qexp[vh] = jnp.where((lane >= lo) & (lane < lo + width), qg, jnp.zeros_like(qg))
        m_s[...] = jnp.full_like(m_s, NEG)
        acc_s[...] = jnp.zeros_like(acc_s)

    def update(masked):
        if masked:
            keep = _lane_iota((tq, tk)) <= _row_iota((tq, tk))
        for vh, (grp, _, _, hv) in enumerate(heads):
            kg = k_ref[:, grp * LANES:(grp + 1) * LANES]
            s = lax.dot_general(qexp[vh], kg, (((1,), (1,)), ((), ())), preferred_element_type=F32)
            if masked:
                s = jnp.where(keep, s, NEG)
            m_old = m_s[vh]
            m_new = jnp.maximum(m_old, jnp.max(s, axis=1, keepdims=True))
            alpha = jnp.exp(m_old - m_new)
            p = jnp.exp(s - jnp.concatenate([m_new] * (tk // LANES), axis=1))
            pv = jnp.dot(p.astype(BF16), v_ref[:, hv * LANES:(hv + 1) * LANES], preferred_element_type=F32)
            acc_s[vh] = alpha * acc_s[vh] + pv
            m_s[vh] = m_new

    @pl.when(ki < qi)
    def _():
        update(False)

    @pl.when(ki == qi)
    def _():
        update(True)
        def normalised(vh):
            a = acc_s[vh]
            return (a / pltpu.roll(a, dv, axis=1))[:, :dv]

        outs = []
        if diff_lam_init is None:
            for vh in range(len(heads)):
                outs.append(normalised(vh))
        else:
            aux = aux_ref[...]
            lam = (jnp.exp(jnp.sum(aux[0:1] * aux[1:2], axis=1, keepdims=True))
                   - jnp.exp(jnp.sum(aux[2:3] * aux[3:4], axis=1, keepdims=True)) + diff_lam_init)
            gain = aux[4:5, :dv]
            for h in range(N_HEADS):
                o = normalised(2 * h) - lam * normalised(2 * h + 1)
                ms = jnp.mean(o * o, axis=1, keepdims=True)
                outs.append(o * lax.rsqrt(ms + 1e-5) * gain * (1.0 - diff_lam_init))
        o_ref[...] = jnp.concatenate(outs, axis=1).astype(o_ref.dtype)


def _flash(q, k, v, aux, batch, heads, diff_lam_init):
    t, wq = q.shape
    s = t // batch
    tq = TQ
    nq = s // tq
    pairs = [(a, b) for a in range(nq) for b in range(a + 1)]
    qi_tab = jnp.asarray([p[0] for p in pairs], I32)
    ki_tab = jnp.asarray([p[1] for p in pairs], I32)
    nv = len(heads)
    dv = GROUP_WIDTH // N_HEADS
    grid_spec = pltpu.PrefetchScalarGridSpec(
        num_scalar_prefetch=2,
        grid=(batch, len(pairs)),
        in_specs=[pl.BlockSpec((tq, wq), lambda b, i, qt, kt: (b * nq + qt[i], 0)),
                  pl.BlockSpec((tq, wq), lambda b, i, qt, kt: (b * nq + kt[i], 0)),
                  pl.BlockSpec((tq, V_EXT), lambda b, i, qt, kt: (b * nq + kt[i], 0)),
                  pl.BlockSpec(aux.shape, lambda b, i, qt, kt: (0, 0))],
        out_specs=pl.BlockSpec((tq, 256), lambda b, i, qt, kt: (b * nq + qt[i], 0)),
        scratch_shapes=[pltpu.VMEM((nv, tq, LANES), BF16), pltpu.VMEM((nv, tq, LANES), F32),
                        pltpu.VMEM((nv, tq, LANES), F32)])
    return pl.pallas_call(
        functools.partial(_flash_kernel, heads=heads, diff_lam_init=diff_lam_init),
        grid_spec=grid_spec,
        out_shape=jax.ShapeDtypeStruct((t, 256), BF16),
        compiler_params=_cparams(("parallel", "arbitrary")),
    )(qi_tab, ki_tab, q, k, v, aux)


DIFF_HEADS_SPEC = tuple(((2 * h + m) // 4, 32 * ((2 * h + m) % 4), 32, h)
                        for h in range(N_HEADS) for m in range(2))
MLA_HEADS_SPEC = tuple((h, 0, LANES, h) for h in range(N_HEADS))


def _layer_norm(z, g, b):
    mu = jnp.mean(z, axis=1, keepdims=True)
    zc = z - mu
    var = jnp.mean(zc * zc, axis=1, keepdims=True)
    return zc * lax.rsqrt(var + NORM_EPS) * g + b


def _outproj_kernel(a_ref, p_ref, c_ref, d_ref, x_ref, w_ref, g_ref, b_ref, rwh_ref, rwl_ref, rb_ref, tri_ref,
                    h_ref, idx_ref, gate_ref, rank_ref, cnt_ref):
    y = jnp.zeros(h_ref.shape, F32)
    for i, part in enumerate((a_ref, p_ref, c_ref, d_ref)):
        y = y + jnp.dot(part[...], w_ref[i * GROUP_WIDTH:(i + 1) * GROUP_WIDTH, :],
                        preferred_element_type=F32)
    h = _layer_norm(DEEPNORM_ALPHA * x_ref[...] + y, g_ref[...], b_ref[...])
    h_ref[...] = h

    hi = h.astype(BF16)
    lo = (h - hi.astype(F32)).astype(BF16)
    logits = (jnp.dot(hi, rwh_ref[...], preferred_element_type=F32)
              + jnp.dot(lo, rwh_ref[...], preferred_element_type=F32)
              + jnp.dot(hi, rwl_ref[...], preferred_element_type=F32) + rb_ref[...])
    lane = _lane_iota(logits.shape)
    logits = jnp.where(lane < N_EXPERTS, logits, NEG)
    idx_out = jnp.zeros(logits.shape, I32)
    chosen = jnp.zeros(logits.shape, F32)
    top, picks = [], []
    for r in range(TOP_K):
        m = jnp.max(logits, axis=1, keepdims=True)
        idx = jnp.min(jnp.where(logits == m, lane, LANES), axis=1, keepdims=True)
        idx_out = jnp.where(lane == r, idx, idx_out)
        pick = lane == idx
        chosen = jnp.where(pick, 1.0, chosen)
        logits = jnp.where(pick, NEG, logits)
        top.append(m)
        picks.append(pick)
    es = [jnp.exp(m - top[0]) for m in top]
    denom = es[0] + es[1] + es[2] + es[3]
    gates = jnp.zeros(logits.shape, F32)
    for r in range(TOP_K):
        gates = jnp.where(lane == r, es[r] / denom, gates)
    idx_ref[...] = idx_out
    gate_ref[...] = gates

    seen = jnp.dot(tri_ref[...], chosen.astype(BF16), preferred_element_type=F32)
    rank = jnp.zeros(logits.shape, F32)
    for r in range(TOP_K):
        before = jnp.sum(jnp.where(picks[r], seen, 0.0), axis=1, keepdims=True) - 1.0
        rank = jnp.where(lane == r, before, rank)
    rank_ref[...] = rank.astype(I32)
    cnt_ref[0] = seen[seen.shape[0] - 8:, :].astype(I32)


def _outproj(parts, x2, w_out, g, b, rwh, rwl, rb, tri):
    t = x2.shape[0]
    tm = TM_OUT
    row = lambda w: pl.BlockSpec((tm, w), lambda i: (i, 0))
    full = lambda a: pl.BlockSpec(a.shape, lambda i: (0,) * a.ndim)
    return pl.pallas_call(
        _outproj_kernel,
        grid=(t // tm,),
        in_specs=[row(256)] * 4 + [row(D_MODEL), full(w_out), full(g), full(b), full(rwh), full(rwl), full(rb),
                                   full(tri)],
        out_specs=[row(D_MODEL), row(LANES), row(LANES), row(LANES),
                   pl.BlockSpec((1, 8, LANES), lambda i: (i, 0, 0))],
        out_shape=[jax.ShapeDtypeStruct((t, D_MODEL), F32), jax.ShapeDtypeStruct((t, LANES), I32),
                   jax.ShapeDtypeStruct((t, LANES), F32), jax.ShapeDtypeStruct((t, LANES), I32),
                   jax.ShapeDtypeStruct((t // tm, 8, LANES), I32)],
        compiler_params=_cparams(("parallel",)),
    )(*parts, x2, w_out, g, b, rwh, rwl, rb, tri)


def _route(top_idx, rank, tile_cnt, bm, tm):
    t = top_idx.shape[0]
    n_tiles = t // tm
    tile_cnt = tile_cnt[:, :N_EXPERTS]
    tile_base = jnp.cumsum(tile_cnt, axis=0) - tile_cnt
    counts = jnp.sum(tile_cnt, axis=0)
    nblk_e = (counts + bm - 1) // bm
    blk_end = jnp.cumsum(nblk_e)
    blk_start = blk_end - nblk_e
    offs = blk_start[None, :] * bm + tile_base
    onehot = top_idx.reshape(n_tiles, tm, TOP_K, 1) == jnp.arange(N_EXPERTS, dtype=I32)
    dest = rank.reshape(n_tiles, tm, TOP_K) + jnp.sum(jnp.where(onehot, offs[:, None, None, :], 0), axis=-1)
    n_blocks = (t * TOP_K) // bm + N_EXPERTS
    blk = jnp.arange(n_blocks, dtype=I32)
    block_e = jnp.minimum(jnp.sum(blk[:, None] >= blk_end[None, :], axis=1), N_EXPERTS - 1).astype(I32)
    nb = blk_end[-1:].astype(I32)
    cnt = jnp.clip(counts[block_e] - (blk - blk_start[block_e]) * bm, 0, bm)
    cnt = jnp.where(blk < nb[0], cnt, 0).astype(I32)
    return dest.reshape(t * TOP_K).astype(I32), block_e, nb, cnt


def _dispatch_kernel(cnt_ref, dest_ref, h_ref, xs_hbm, stage, zbuf, sem, *, tm, n_steps, bm, n_blocks):
    i = pl.program_id(0)
    slot = i % 2

    @pl.when(i == 0)
    def _():
        zbuf[...] = jnp.zeros_like(zbuf)

        def zero_blocks(act):
            def body(blk, c):
                @pl.when(cnt_ref[blk] < bm)
                def _():
                    act(pltpu.make_async_copy(zbuf, xs_hbm.at[pl.ds(pl.multiple_of(blk * bm, bm), bm)], sem.at[2]))
                return c
            lax.fori_loop(0, n_blocks, body, 0)

        zero_blocks(lambda cp: cp.start())
        zero_blocks(lambda cp: cp.wait())

    stage[slot] = h_ref[...]

    def copy(r, dst, s):
        return pltpu.make_async_copy(stage.at[s, pl.ds(r, 1)], xs_hbm.at[pl.ds(dst, 1)], sem.at[s])

    for r in range(tm):
        for k in range(TOP_K):
            copy(r, dest_ref[0, 0, r * TOP_K + k], slot).start()

    def drain(s):
        for _ in range(tm * TOP_K):
            copy(0, 0, s).wait()

    @pl.when(i >= 1)
    def _():
        drain(1 - slot)

    @pl.when(i == n_steps - 1)
    def _():
        drain(slot)


def _dispatch(h, dest, cnt, n_rows):
    t = h.shape[0]
    tm = TM_DISP
    bm = BM_MOE
    n_steps = t // tm
    grid_spec = pltpu.PrefetchScalarGridSpec(
        num_scalar_prefetch=1,
        grid=(n_steps,),
        in_specs=[pl.BlockSpec((1, 1, tm * TOP_K), lambda i, *_: (i, 0, 0), memory_space=pltpu.SMEM),
                  pl.BlockSpec((tm, D_MODEL), lambda i, *_: (i, 0))],
        out_specs=pl.BlockSpec(memory_space=pl.ANY),
        scratch_shapes=[pltpu.VMEM((2, tm, D_MODEL), F32), pltpu.VMEM((bm, D_MODEL), F32),
                        pltpu.SemaphoreType.DMA((3,))])
    return pl.pallas_call(
        functools.partial(_dispatch_kernel, tm=tm, n_steps=n_steps, bm=bm, n_blocks=n_rows // bm),
        grid_spec=grid_spec,
        out_shape=jax.ShapeDtypeStruct((n_rows, D_MODEL), F32),
        compiler_params=_cparams(("arbitrary",)),
    )(cnt, dest.reshape(n_steps, 1, tm * TOP_K), h)


def _expert_kernel(be_ref, nb_ref, x_ref, wgu_ref, bgu_ref, wd_ref, bd_ref, y_ref, wgu_bf, wd_bf):
    i = pl.program_id(0)
    live = i < nb_ref[0]
    first_of_expert = (i == 0) | (be_ref[i] != be_ref[jnp.maximum(i - 1, 0)])

    @pl.when(live & first_of_expert)
    def _():
        wgu_bf[...] = wgu_ref[0].astype(BF16)
        wd_bf[...] = wd_ref[0].astype(BF16)

    @pl.when(live)
    def _():
        x = x_ref[...].astype(BF16)
        acc = jnp.zeros(y_ref.shape, F32)
        for j in range(D_EXPERT // HC_MOE):
            lo = j * HC_MOE
            gate = (jnp.dot(x, wgu_bf[:, lo:lo + HC_MOE], preferred_element_type=F32)
                    + bgu_ref[0, :, lo:lo + HC_MOE])
            up = (jnp.dot(x, wgu_bf[:, D_EXPERT + lo:D_EXPERT + lo + HC_MOE], preferred_element_type=F32)
                  + bgu_ref[0, :, D_EXPERT + lo:D_EXPERT + lo + HC_MOE])
            gate = jnp.minimum(gate, SWIGLU_LIMIT)
            up = jnp.clip(up, -SWIGLU_LIMIT, SWIGLU_LIMIT)
            act = (up + 1.0) * gate * (1.0 / (1.0 + jnp.exp(-SWIGLU_ALPHA * gate)))
            acc = acc + jnp.dot(act.astype(BF16), wd_bf[lo:lo + HC_MOE, :], preferred_element_type=F32)
        y_ref[...] = acc + bd_ref[0]

    @pl.when(jnp.logical_not(live))
    def _():
        y_ref[...] = jnp.zeros_like(y_ref)


def _experts(xs, block_e, nb, wgu, bgu, wd, bd, layer):
    bm = BM_MOE
    n_blocks = xs.shape[0] // bm
    per_expert = lambda *dims: pl.BlockSpec((None, 1) + dims, lambda i, be, *_: (layer, be[i], 0, 0))
    grid_spec = pltpu.PrefetchScalarGridSpec(
        num_scalar_prefetch=2,
        grid=(n_blocks,),
        in_specs=[pl.BlockSpec((bm, D_MODEL), lambda i, *_: (i, 0)),
                  per_expert(D_MODEL, 2 * D_EXPERT), per_expert(1, 2 * D_EXPERT),
                  per_expert(D_EXPERT, D_MODEL), per_expert(1, D_MODEL)],
        out_specs=pl.BlockSpec((bm, D_MODEL), lambda i, *_: (i, 0)),
        scratch_shapes=[pltpu.VMEM((D_MODEL, 2 * D_EXPERT), BF16), pltpu.VMEM((D_EXPERT, D_MODEL), BF16)])
    return pl.pallas_call(
        _expert_kernel,
        grid_spec=grid_spec,
        out_shape=jax.ShapeDtypeStruct(xs.shape, F32),
        compiler_params=_cparams(("arbitrary",)),
    )(block_e, nb, xs, wgu, bgu, wd, bd)


def _combine_kernel(dest_ref, nxt_ref, gate_ref, h_ref, g_ref, b_ref, ys_hbm, o_ref, buf, sem, *, tm, n_steps):
    i = pl.program_id(0)
    slot = i % 2

    def copy(src, r, k, s):
        return pltpu.make_async_copy(ys_hbm.at[pl.ds(src, 1)], buf.at[s, k, pl.ds(r, 1)], sem.at[s])

    def issue(idx_ref, s):
        for r in range(tm):
            for k in range(TOP_K):
                copy(idx_ref[0, 0, r * TOP_K + k], r, k, s).start()

    @pl.when(i == 0)
    def _():
        issue(dest_ref, 0)

    @pl.when(i + 1 < n_steps)
    def _():
        issue(nxt_ref, 1 - slot)

    for _ in range(tm * TOP_K):
        copy(0, 0, 0, slot).wait()

    gates = gate_ref[...]
    f = jnp.zeros(o_ref.shape, F32)
    for k in range(TOP_K):
        f = f + gates[:, k:k + 1] * buf[slot, k]
    o_ref[...] = _layer_norm(DEEPNORM_ALPHA * h_ref[...] + f, g_ref[...], b_ref[...])


def _combine(ys, dest, gates, h, g, b):
    t = h.shape[0]
    tm = TM_COMB
    n_steps = t // tm
    row = lambda w: pl.BlockSpec((tm, w), lambda i: (i, 0))
    full = lambda a: pl.BlockSpec(a.shape, lambda i: (0,) * a.ndim)
    idx_spec = lambda f: pl.BlockSpec((1, 1, tm * TOP_K), f, memory_space=pltpu.SMEM)
    dest3 = dest.reshape(n_steps, 1, tm * TOP_K)
    return pl.pallas_call(
        functools.partial(_combine_kernel, tm=tm, n_steps=n_steps),
        grid=(n_steps,),
        in_specs=[idx_spec(lambda i: (i, 0, 0)), idx_spec(lambda i: (jnp.minimum(i + 1, n_steps - 1), 0, 0)),
                  row(LANES), row(D_MODEL), full(g), full(b), pl.BlockSpec(memory_space=pl.ANY)],
        out_specs=row(D_MODEL),
        out_shape=jax.ShapeDtypeStruct((t, D_MODEL), F32),
        scratch_shapes=[pltpu.VMEM((2, TOP_K, tm, D_MODEL), F32), pltpu.SemaphoreType.DMA((2,))],
        compiler_params=_cparams(("arbitrary",)),
    )(dest3, dest3, gates, h, g, b, ys)


def _pad_cols(w, width):
    return jnp.pad(w, ((0, 0), (0, width - w.shape[1])))


def _pack_layer(w_in, gla_w_gate, gla_b_gate, gla_norm, pool_w, pool_scale, diff_lq1, diff_lk1,
                diff_lq2, diff_lk2, diff_norm, mla_q_norm, mla_w_uq, mla_kv_norm, mla_w_ukv,
                router_w, router_b):
    offs = [0, 128, 256, 512, 768, 784, 1040, 1296, 1552, 1808, 2000, 2160]
    seg = lambda i: w_in[:, offs[i]:offs[i + 1]]
    dvh = GROUP_WIDTH // N_HEADS
    spread_heads = lambda w: jnp.concatenate(
        [_pad_cols(w[:, h * dvh:(h + 1) * dvh], LANES) for h in range(N_HEADS)], axis=1)
    w_big = jnp.concatenate(
        [seg(0), seg(1), seg(2), seg(3), _pad_cols(seg(4), LANES), seg(5), seg(6), seg(7), spread_heads(seg(8)),
         _pad_cols(seg(9), 256), _pad_cols(seg(10), 256)], axis=1).astype(BF16)
    wg = jnp.pad(gla_w_gate, ((0, LANES - GLA_GATE_RANK), (0, 0))).astype(BF16)
    bg = gla_b_gate.reshape(1, LANES)
    gnorm = jnp.tile(gla_norm, N_HEADS).reshape(1, 256)
    eye = jnp.eye(N_HEADS, dtype=F32)
    ind = jnp.kron(eye, jnp.ones((GLA_DK, GLA_DV), F32)).astype(BF16)
    tri = jnp.tril(jnp.ones((GLA_CHUNK, GLA_CHUNK), F32)).astype(BF16)
    pool_bd = jax.scipy.linalg.block_diag(*[pool_w[g] for g in range(4)]).astype(BF16)
    pscale = pool_scale.reshape(1, 256)
    aux = jnp.zeros((8, LANES), F32)
    aux = aux.at[0, :DIFF_DH].set(diff_lq1).at[1, :DIFF_DH].set(diff_lk1)
    aux = aux.at[2, :DIFF_DH].set(diff_lq2).at[3, :DIFF_DH].set(diff_lk2)
    aux = aux.at[4, :2 * DIFF_DH].set(diff_norm)
    qn = _pad_cols(mla_q_norm.reshape(1, MLA_Q_RANK), 256)
    kvn = mla_kv_norm.reshape(1, MLA_KV_RANK)
    dqk = MLA_NOPE + MLA_ROPE
    wuq = jnp.concatenate([_pad_cols(mla_w_uq[:, h * dqk:(h + 1) * dqk], LANES) for h in range(N_HEADS)], axis=1)
    wuq = jnp.pad(wuq, ((0, 256 - MLA_Q_RANK), (0, 0))).astype(BF16)
    dkv = MLA_NOPE + MLA_V
    wk = jnp.concatenate([_pad_cols(mla_w_ukv[:, h * dkv:h * dkv + MLA_NOPE], LANES) for h in range(N_HEADS)], axis=1)
    wv = jnp.concatenate([mla_w_ukv[:, h * dkv + MLA_NOPE:(h + 1) * dkv] for h in range(N_HEADS)], axis=1)
    wukv = jnp.concatenate([wk, spread_heads(wv)], axis=1).astype(BF16)
    sel = jnp.zeros((LANES, 4 * LANES), F32)
    for h in range(N_HEADS):
        sel = sel.at[jnp.arange(MLA_ROPE), h * LANES + MLA_NOPE + jnp.arange(MLA_ROPE)].set(1.0)
    sel = sel.astype(BF16)
    rw = _pad_cols(router_w, LANES)
    rwh = rw.astype(BF16)
    rwl = (rw - rwh.astype(F32)).astype(BF16)
    rb = _pad_cols(router_b.reshape(1, N_EXPERTS), LANES)
    return dict(w_big=w_big, wg=wg, bg=bg, gnorm=gnorm, ind=ind, tri=tri, pool_bd=pool_bd, pscale=pscale,
                aux=aux, qn=qn, kvn=kvn, wuq=wuq, wukv=wukv, sel=sel, rwh=rwh, rwl=rwl, rb=rb)


def kernel(x, positions, w_in, gla_w_gate, gla_b_gate, gla_norm, pool_w, pool_scale, diff_lq1, diff_lk1,
           diff_lq2, diff_lk2, diff_norm, mla_q_norm, mla_w_uq, mla_kv_norm, mla_w_ukv, w_out, ln1_g, ln1_b,
           router_w, router_b, w_gate_up, b_gate_up, w_down, b_down, ln2_g, ln2_b):
    batch, seq, d = x.shape
    t = batch * seq
    assert d == D_MODEL and seq % TQ == 0 and t % TM_PROJ == 0 and (t * TOP_K) % BM_MOE == 0

    inv = 1.0 / (ROPE_THETA ** (jnp.arange(0, DIFF_DH, 2, dtype=F32) / DIFF_DH))
    ang = positions.astype(F32).reshape(t, 1) * inv[None, :]
    cos, sin = jnp.cos(ang), jnp.sin(ang)
    cos128 = jnp.tile(jnp.concatenate([cos, cos], axis=1), (1, 4))
    sin128 = jnp.tile(jnp.concatenate([-sin, sin], axis=1), (1, 4))

    tri_out = jnp.tril(jnp.ones((TM_OUT, TM_OUT), F32)).astype(BF16)

    h = x.reshape(t, d)
    for l in range(DEPTH):
        p = _pack_layer(w_in[l], gla_w_gate[l], gla_b_gate[l], gla_norm[l], pool_w[l], pool_scale[l],
                        diff_lq1[l], diff_lk1[l], diff_lq2[l], diff_lk2[l], diff_norm[l], mla_q_norm[l],
                        mla_w_uq[l], mla_kv_norm[l], mla_w_ukv[l], router_w[l], router_b[l])
        gla_in, pool_in, dq, dk, dv, mq, mk, mv = _proj(
            h, p['w_big'], cos128, sin128, p['qn'], p['kvn'], p['wuq'], p['wukv'], p['sel'])
        gla_out = _gla(gla_in, p['wg'], p['bg'], p['gnorm'], p['ind'], p['tri'], batch)
        pool_out = _pool(pool_in, p['pool_bd'], p['pscale'], batch)
        lam_init = 0.8 - 0.6 * math.exp(-0.3 * l)
        diff_out = _flash(dq, dk, dv, p['aux'], batch, DIFF_HEADS_SPEC, lam_init)
        mla_out = _flash(mq, mk, mv, p['aux'], batch, MLA_HEADS_SPEC, None)
        h1, top_idx, gates, rank, tile_cnt = _outproj(
            (gla_out, pool_out, diff_out, mla_out), h, w_out[l].astype(BF16), ln1_g[l].reshape(1, d),
            ln1_b[l].reshape(1, d), p['rwh'], p['rwl'], p['rb'], tri_out)
        dest, block_e, nb, cnt = _route(top_idx[:, :TOP_K], rank[:, :TOP_K], tile_cnt[:, 7, :], BM_MOE, TM_OUT)
        n_rows = t * TOP_K + N_EXPERTS * BM_MOE
        xs = _dispatch(h1, dest, cnt, n_rows)
        ys = _experts(xs, block_e, nb, w_gate_up, b_gate_up.reshape(DEPTH, N_EXPERTS, 1, 2 * D_EXPERT),
                      w_down, b_down.reshape(DEPTH, N_EXPERTS, 1, D_MODEL), l)
        h = _combine(ys, dest, gates, h1, ln2_g[l].reshape(1, d), ln2_b[l].reshape(1, d))
    return h.reshape(batch, seq, d)
```

```python
import functools
import math

import jax
import jax.numpy as jnp
from jax import lax
from jax.experimental import pallas as pl
from jax.experimental.pallas import tpu as pltpu

F32 = jnp.float32
BF16 = jnp.bfloat16
I32 = jnp.int32

D_MODEL = 1024
DEPTH = 2
GROUP_WIDTH = 256
N_HEADS = 4
GLA_DK = 32
GLA_DV = 64
GLA_GATE_RANK = 16
GLA_GATE_NORM = 16.0
POOL_WINDOWS = (2, 4, 8, 16)
POOL_CH = 64
DIFF_DH = 32
MLA_Q_RANK = 192
MLA_KV_RANK = 128
MLA_NOPE = 64
MLA_ROPE = 32
MLA_V = 64
ROPE_THETA = 10000.0
N_EXPERTS = 32
TOP_K = 4
D_EXPERT = 1024
SWIGLU_LIMIT = 7.0
SWIGLU_ALPHA = 1.702
NORM_EPS = 1e-5
DEEPNORM_ALPHA = (2 * DEPTH) ** 0.25

LANES = 128
MXU_WIDTH = 256
VMEM_LIMIT = 56 * 1024 * 1024

C_GLA = 0
C_POOL = 896
C_DQ = 1152
C_DK = 1408
C_DV = 1664
C_MQA = 2176
C_MKVA = 2432
C_END = 2688
GLA_W = C_POOL - C_GLA
V_EXT = 4 * LANES

NEG = -1e30
LOG2E = math.log2(math.e)

TM_PROJ = 512
TG_GLA = 512
GLA_CHUNK = 64
GLA_SUB = 16
TQ = 512
TM_OUT = 512
BM_MOE = 512
HC_MOE = 512
TM_DISP = 128
TM_COMB = 128


def _cparams(sem):
    return pltpu.CompilerParams(dimension_semantics=sem, vmem_limit_bytes=VMEM_LIMIT)


def _lane_iota(shape):
    return lax.broadcasted_iota(I32, shape, len(shape) - 1)


def _row_iota(shape):
    return lax.broadcasted_iota(I32, shape, len(shape) - 2)


def _rope(x, cos, sin_signed):
    w = x.shape[-1]
    from_low = pltpu.roll(x, 16, axis=1)
    from_high = pltpu.roll(x, w - 16, axis=1)
    swapped = jnp.where((_lane_iota(x.shape) % 32) >= 16, from_low, from_high)
    return x * cos + swapped * sin_signed


def _proj_kernel(x_ref, w_ref, cos_ref, sin_ref, qn_ref, kvn_ref, wuq_ref, wukv_ref, sel_ref,
                 gla_ref, pool_ref, dq_ref, dk_ref, dv_ref, mq_ref, mk_ref, mv_ref):
    xb = x_ref[...].astype(BF16)

    def seg(lo, hi):
        return jnp.dot(xb, w_ref[:, lo:hi], preferred_element_type=F32)

    gla_ref[...] = seg(C_GLA, C_POOL)
    pool_ref[...] = seg(C_POOL, C_DQ)

    cos = cos_ref[...]
    sin = sin_ref[...]
    cos2 = jnp.concatenate([cos, cos], axis=1)
    sin2 = jnp.concatenate([sin, sin], axis=1)
    dq_ref[...] = (_rope(seg(C_DQ, C_DK), cos2, sin2) * (LOG2E * DIFF_DH ** -0.5)).astype(BF16)
    dk_ref[...] = _rope(seg(C_DK, C_DV), cos2, sin2).astype(BF16)
    ones_cols = ((_lane_iota((1, V_EXT)) % LANES) >= GROUP_WIDTH // N_HEADS).astype(F32)
    dv_ref[...] = (seg(C_DV, C_MQA) + ones_cols).astype(BF16)

    mqa = seg(C_MQA, C_MKVA)
    ms = jnp.sum(mqa * mqa, axis=1, keepdims=True) * (1.0 / MLA_Q_RANK)
    cq = (mqa * lax.rsqrt(ms + 1e-6) * qn_ref[...]).astype(BF16)
    mq = jnp.dot(cq, wuq_ref[...], preferred_element_type=F32)
    cos4 = jnp.concatenate([cos2, cos2], axis=1)
    sin4 = jnp.concatenate([sin2, sin2], axis=1)
    lane = _lane_iota(mq.shape) % LANES
    is_rope = (lane >= MLA_NOPE) & (lane < MLA_NOPE + MLA_ROPE)
    mq = jnp.where(is_rope, _rope(mq, cos4, sin4), mq)
    mq_ref[...] = (mq * (LOG2E * (MLA_NOPE + MLA_ROPE) ** -0.5)).astype(BF16)

    mkva = seg(C_MKVA, C_END)
    ckv_raw = mkva[:, :LANES]
    msk = jnp.sum(ckv_raw * ckv_raw, axis=1, keepdims=True) * (1.0 / MLA_KV_RANK)
    ckv = (ckv_raw * lax.rsqrt(msk + 1e-6) * kvn_ref[...]).astype(BF16)
    kv = jnp.dot(ckv, wukv_ref[...], preferred_element_type=F32)
    k_rope = _rope(mkva[:, LANES:], cos, sin).astype(BF16)
    placed = jnp.dot(k_rope, sel_ref[...], preferred_element_type=F32)
    mk_ref[...] = (kv[:, :4 * LANES] + placed).astype(BF16)
    mv_ref[...] = (kv[:, 4 * LANES:] + ones_cols).astype(BF16)


def _proj(x2, w_big, cos128, sin128, qn, kvn, wuq, wukv, sel):
    t = x2.shape[0]
    tm = TM_PROJ
    row = lambda w: pl.BlockSpec((tm, w), lambda i: (i, 0))
    full = lambda a: pl.BlockSpec(a.shape, lambda i: (0,) * a.ndim)
    out_shapes = [
        jax.ShapeDtypeStruct((t, GLA_W), F32), jax.ShapeDtypeStruct((t, 256), F32),
        jax.ShapeDtypeStruct((t, 256), BF16), jax.ShapeDtypeStruct((t, 256), BF16),
        jax.ShapeDtypeStruct((t, V_EXT), BF16), jax.ShapeDtypeStruct((t, 512), BF16),
        jax.ShapeDtypeStruct((t, 512), BF16), jax.ShapeDtypeStruct((t, V_EXT), BF16)]
    return pl.pallas_call(
        _proj_kernel,
        grid=(t // tm,),
        in_specs=[row(D_MODEL), full(w_big), row(LANES), row(LANES), full(qn), full(kvn),
                  full(wuq), full(wukv), full(sel)],
        out_specs=[row(s.shape[1]) for s in out_shapes],
        out_shape=out_shapes,
        compiler_params=_cparams(("parallel",)),
    )(x2, w_big, cos128, sin128, qn, kvn, wuq, wukv, sel)


def _split3(x):
    hi = x.astype(BF16)
    r1 = x - hi.astype(F32)
    mid = r1.astype(BF16)
    lo = (r1 - mid.astype(F32)).astype(BF16)
    return hi, mid, lo


def _gla_kernel(blk_ref, wg_ref, bg_ref, norm_ref, ind_ref, tri_ref, out_ref, st_ref, *, n_chunks):
    c_len = GLA_CHUNK

    @pl.when(pl.program_id(1) == 0)
    def _():
        st_ref[...] = jnp.zeros_like(st_ref)

    same_head = (_row_iota((256, LANES)) // GLA_DV) == (_lane_iota((256, LANES)) // GLA_DK)
    ind = ind_ref[...]
    tri = tri_ref[...]
    rows = _row_iota((8, LANES))

    def chunk(c, carry):
        r0 = pl.multiple_of(c * c_len, c_len)
        blk = blk_ref[pl.ds(r0, c_len), :]
        q = blk[:, 0:128] * (GLA_DK ** -0.5)
        k = blk[:, 128:256]
        v = blk[:, 256:512]
        g = blk[:, 512:768]
        gr = blk[:, 768:896]
        logit = jnp.dot(gr.astype(BF16), wg_ref[...], preferred_element_type=F32) + bg_ref[...]
        log_a = (jnp.minimum(logit, 0.0) - jnp.log(1.0 + jnp.exp(-jnp.abs(logit)))) * (1.0 / GLA_GATE_NORM)
        hi, mid, lo = _split3(log_a)
        b = (jnp.dot(tri, hi, preferred_element_type=F32) + jnp.dot(tri, mid, preferred_element_type=F32)
             + jnp.dot(tri, lo, preferred_element_type=F32))

        st = st_ref[...]
        o_subs = []
        for sc in range(c_len // GLA_SUB):
            lo = sc * GLA_SUB
            bs = b[lo - 1:lo, :] if sc else jnp.zeros((1, LANES), F32)
            b_s, q_s, k_s, v_s = (a[lo:lo + GLA_SUB, :] for a in (b, q, k, v))
            qd = (q_s * jnp.exp(b_s - bs)).astype(BF16)
            o_s = lax.dot_general(qd, st.astype(BF16), (((1,), (1,)), ((), ())), preferred_element_type=F32)
            tiles = [o_s[8 * t:8 * (t + 1), :] for t in range(GLA_SUB // 8)]
            e_tiles, owners = [], []
            for j in range(GLA_SUB):
                for t in range(j // 8, GLA_SUB // 8):
                    rs = slice(8 * t, 8 * (t + 1))
                    e = jnp.exp(b_s[rs, :] - b_s[j:j + 1, :]) * (q_s[rs, :] * k_s[j:j + 1, :])
                    if t == j // 8 and j % 8:
                        e = jnp.where(rows >= j % 8, e, 0.0)
                    e_tiles.append(e)
                    owners.append((j, t))
            p_all = jnp.dot(jnp.concatenate(e_tiles, axis=0).astype(BF16), ind, preferred_element_type=F32)
            for n, (j, t) in enumerate(owners):
                tiles[t] = tiles[t] + p_all[8 * n:8 * (n + 1), :] * v_s[j:j + 1, :]
            o_subs.extend(tiles)
            b_end = b_s[GLA_SUB - 1:GLA_SUB, :]
            kd = (k_s * jnp.exp(b_end - b_s)).astype(BF16)
            upd = lax.dot_general(v_s.astype(BF16), kd, (((0,), (0,)), ((), ())), preferred_element_type=F32)
            st = st * jnp.exp(b_end - bs) + jnp.where(same_head, upd, 0.0)
        st_ref[...] = st
        o = jnp.concatenate(o_subs, axis=0)

        o2 = o * o
        head = _lane_iota(o.shape) // GLA_DV
        scale = jnp.zeros_like(o)
        for h in range(N_HEADS):
            ms = jnp.sum(jnp.where(head == h, o2, 0.0), axis=1, keepdims=True) * (1.0 / GLA_DV)
            scale = jnp.where(head == h, lax.rsqrt(ms + 1e-6), scale)
        silu = g / (1.0 + jnp.exp(-g))
        out_ref[pl.ds(r0, c_len), :] = (o * scale * norm_ref[...] * silu).astype(out_ref.dtype)
        return carry

    lax.fori_loop(0, n_chunks, chunk, 0, unroll=2)


def _gla(gla_in, wg, bg, norm, ind, tri, batch):
    t = gla_in.shape[0]
    s = t // batch
    tg = TG_GLA
    nblk = s // tg
    full = lambda a: pl.BlockSpec(a.shape, lambda b, i: (0,) * a.ndim)
    return pl.pallas_call(
        functools.partial(_gla_kernel, n_chunks=tg // GLA_CHUNK),
        grid=(batch, nblk),
        in_specs=[pl.BlockSpec((tg, GLA_W), lambda b, i: (b * nblk + i, 0)),
                  full(wg), full(bg), full(norm), full(ind), full(tri)],
        out_specs=pl.BlockSpec((tg, 256), lambda b, i: (b * nblk + i, 0)),
        out_shape=jax.ShapeDtypeStruct((t, 256), BF16),
        scratch_shapes=[pltpu.VMEM((256, LANES), F32)],
        compiler_params=_cparams(("parallel", "arbitrary")),
    )(gla_in, wg, bg, norm, ind, tri)


def _pool_kernel(u_ref, w_ref, scale_ref, out_ref):
    u = u_ref[...]
    t = _row_iota(u.shape)

    def shifted(x, k):
        return jnp.where(t >= k, pltpu.roll(x, k, axis=0), 0.0)

    sums = []
    acc = u
    for k in (1, 2, 4, 8):
        acc = acc + shifted(acc, k)
        sums.append(acc)
    group = _lane_iota(u.shape) // POOL_CH
    tf = (t + 1).astype(F32)
    mean = jnp.zeros_like(u)
    for gi, w in enumerate(POOL_WINDOWS):
        mean = jnp.where(group == gi, sums[gi] / jnp.minimum(tf, float(w)), mean)
    pooled = (mean - u).astype(BF16)
    y = jnp.dot(pooled, w_ref[...], preferred_element_type=F32) * scale_ref[...]
    out_ref[...] = y.astype(out_ref.dtype)


def _pool(pool_in, w_bd, scale, batch):
    t = pool_in.shape[0]
    s = t // batch
    full = lambda a: pl.BlockSpec(a.shape, lambda b: (0,) * a.ndim)
    return pl.pallas_call(
        _pool_kernel,
        grid=(batch,),
        in_specs=[pl.BlockSpec((s, 256), lambda b: (b, 0)), full(w_bd), full(scale)],
        out_specs=pl.BlockSpec((s, 256), lambda b: (b, 0)),
        out_shape=jax.ShapeDtypeStruct((t, 256), BF16),
        compiler_params=_cparams(("parallel",)),
    )(pool_in, w_bd, scale)


def _flash_kernel(qi_tab, ki_tab, q_ref, k_ref, v_ref, aux_ref, o_ref, qexp, m_s, acc_s,
                  *, heads, diff_lam_init):
    step = pl.program_id(1)
    qi = qi_tab[step]
    ki = ki_tab[step]
    tq = q_ref.shape[0]
    tk = k_ref.shape[0]
    dv = GROUP_WIDTH // N_HEADS

    @pl.when(ki == 0)
    def _():
        for vh, (grp, lo, width, _) in enumerate(heads):
            qg = q_ref[:, grp * LANES:(grp + 1) * LANES]
            lane = _lane_iota(qg.shape)
            qexp[vh] = jnp.where((lane >= lo) & (lane < lo + width), qg, jnp.zeros_like(qg))
        m_s[...] = jnp.full_like(m_s, NEG)
        acc_s[...] = jnp.zeros_like(acc_s)

    def update(masked):
        if masked:
            keep = _lane_iota((tq, tk)) <= _row_iota((tq, tk))
        for vh, (grp, _, _, hv) in enumerate(heads):
            kg = k_ref[:, grp * LANES:(grp + 1) * LANES]
            s = lax.dot_general(qexp[vh], kg, (((1,), (1,)), ((), ())), preferred_element_type=F32)
            if masked:
                s = jnp.where(keep, s, NEG)
            m_old = m_s[vh]
            m_new = jnp.maximum(m_old, jnp.max(s, axis=1, keepdims=True))
            alpha = jnp.exp2(m_old - m_new)
            p = jnp.exp2((s - jnp.concatenate([m_new] * (tk // LANES), axis=1)).astype(BF16))
            pv = jnp.dot(p, v_ref[:, hv * LANES:(hv + 1) * LANES], preferred_element_type=F32)
            acc_s[vh] = alpha * acc_s[vh] + pv
            m_s[vh] = m_new

    @pl.when(ki < qi)
    def _():
        update(False)

    @pl.when(ki == qi)
    def _():
        update(True)
        def normalised(vh):
            a = acc_s[vh]
            return (a / pltpu.roll(a, dv, axis=1))[:, :dv]

        outs = []
        if diff_lam_init is None:
            for vh in range(len(heads)):
                outs.append(normalised(vh))
        else:
            aux = aux_ref[...]
            lam = (jnp.exp(jnp.sum(aux[0:1] * aux[1:2], axis=1, keepdims=True))
                   - jnp.exp(jnp.sum(aux[2:3] * aux[3:4], axis=1, keepdims=True)) + diff_lam_init)
            gain = aux[4:5, :dv]
            for h in range(N_HEADS):
                o = normalised(2 * h) - lam * normalised(2 * h + 1)
                ms = jnp.mean(o * o, axis=1, keepdims=True)
                outs.append(o * lax.rsqrt(ms + 1e-5) * gain * (1.0 - diff_lam_init))
        o_ref[...] = jnp.concatenate(outs, axis=1).astype(o_ref.dtype)


def _flash(q, k, v, aux, batch, heads, diff_lam_init):
    t, wq = q.shape
    s = t // batch
    tq = TQ
    nq = s // tq
    pairs = [(a, b) for a in range(nq) for b in range(a + 1)]
    qi_tab = jnp.asarray([p[0] for p in pairs], I32)
    ki_tab = jnp.asarray([p[1] for p in pairs], I32)
    nv = len(heads)
    dv = GROUP_WIDTH // N_HEADS
    grid_spec = pltpu.PrefetchScalarGridSpec(
        num_scalar_prefetch=2,
        grid=(batch, len(pairs)),
        in_specs=[pl.BlockSpec((tq, wq), lambda b, i, qt, kt: (b * nq + qt[i], 0)),
                  pl.BlockSpec((tq, wq), lambda b, i, qt, kt: (b * nq + kt[i], 0)),
                  pl.BlockSpec((tq, V_EXT), lambda b, i, qt, kt: (b * nq + kt[i], 0)),
                  pl.BlockSpec(aux.shape, lambda b, i, qt, kt: (0, 0))],
        out_specs=pl.BlockSpec((tq, 256), lambda b, i, qt, kt: (b * nq + qt[i], 0)),
        scratch_shapes=[pltpu.VMEM((nv, tq, LANES), BF16), pltpu.VMEM((nv, tq, LANES), F32),
                        pltpu.VMEM((nv, tq, LANES), F32)])
    return pl.pallas_call(
        functools.partial(_flash_kernel, heads=heads, diff_lam_init=diff_lam_init),
        grid_spec=grid_spec,
        out_shape=jax.ShapeDtypeStruct((t, 256), BF16),
        compiler_params=_cparams(("parallel", "arbitrary")),
    )(qi_tab, ki_tab, q, k, v, aux)


DIFF_HEADS_SPEC = tuple(((2 * h + m) // 4, 32 * ((2 * h + m) % 4), 32, h)
                        for h in range(N_HEADS) for m in range(2))
MLA_HEADS_SPEC = tuple((h, 0, LANES, h) for h in range(N_HEADS))


def _layer_norm(z, g, b):
    mu = jnp.mean(z, axis=1, keepdims=True)
    zc = z - mu
    var = jnp.mean(zc * zc, axis=1, keepdims=True)
    return zc * lax.rsqrt(var + NORM_EPS) * g + b


def _outproj_kernel(a_ref, p_ref, c_ref, d_ref, x_ref, w_ref, g_ref, b_ref, rwh_ref, rwl_ref, rb_ref, tri_ref,
                    h_ref, idx_ref, gate_ref, rank_ref, cnt_ref):
    y = jnp.zeros(h_ref.shape, F32)
    for i, part in enumerate((a_ref, p_ref, c_ref, d_ref)):
        y = y + jnp.dot(part[...], w_ref[i * GROUP_WIDTH:(i + 1) * GROUP_WIDTH, :],
                        preferred_element_type=F32)
    h = _layer_norm(DEEPNORM_ALPHA * x_ref[...] + y, g_ref[...], b_ref[...])
    h_ref[...] = h

    hi = h.astype(BF16)
    lo = (h - hi.astype(F32)).astype(BF16)
    logits = (jnp.dot(hi, rwh_ref[...], preferred_element_type=F32)
              + jnp.dot(lo, rwh_ref[...], preferred_element_type=F32)
              + jnp.dot(hi, rwl_ref[...], preferred_element_type=F32) + rb_ref[...])
    lane = _lane_iota(logits.shape)
    logits = jnp.where(lane < N_EXPERTS, logits, NEG)
    idx_out = jnp.zeros(logits.shape, I32)
    chosen = jnp.zeros(logits.shape, F32)
    top, picks = [], []
    for r in range(TOP_K):
        m = jnp.max(logits, axis=1, keepdims=True)
        idx = jnp.min(jnp.where(logits == m, lane, LANES), axis=1, keepdims=True)
        idx_out = jnp.where(lane == r, idx, idx_out)
        pick = lane == idx
        chosen = jnp.where(pick, 1.0, chosen)
        logits = jnp.where(pick, NEG, logits)
        top.append(m)
        picks.append(pick)
    es = [jnp.exp(m - top[0]) for m in top]
    denom = es[0] + es[1] + es[2] + es[3]
    gates = jnp.zeros(logits.shape, F32)
    for r in range(TOP_K):
        gates = jnp.where(lane == r, es[r] / denom, gates)
    idx_ref[...] = idx_out
    gate_ref[...] = gates

    seen = jnp.dot(tri_ref[...], chosen.astype(BF16), preferred_element_type=F32)
    rank = jnp.zeros(logits.shape, F32)
    for r in range(TOP_K):
        before = jnp.sum(jnp.where(picks[r], seen, 0.0), axis=1, keepdims=True) - 1.0
        rank = jnp.where(lane == r, before, rank)
    rank_ref[...] = rank.astype(I32)
    cnt_ref[0] = seen[seen.shape[0] - 8:, :].astype(I32)


def _outproj(parts, x2, w_out, g, b, rwh, rwl, rb, tri):
    t = x2.shape[0]
    tm = TM_OUT
    row = lambda w: pl.BlockSpec((tm, w), lambda i: (i, 0))
    full = lambda a: pl.BlockSpec(a.shape, lambda i: (0,) * a.ndim)
    return pl.pallas_call(
        _outproj_kernel,
        grid=(t // tm,),
        in_specs=[row(256)] * 4 + [row(D_MODEL), full(w_out), full(g), full(b), full(rwh), full(rwl), full(rb),
                                   full(tri)],
        out_specs=[row(D_MODEL), row(LANES), row(LANES), row(LANES),
                   pl.BlockSpec((1, 8, LANES), lambda i: (i, 0, 0))],
        out_shape=[jax.ShapeDtypeStruct((t, D_MODEL), F32), jax.ShapeDtypeStruct((t, LANES), I32),
                   jax.ShapeDtypeStruct((t, LANES), F32), jax.ShapeDtypeStruct((t, LANES), I32),
                   jax.ShapeDtypeStruct((t // tm, 8, LANES), I32)],
        compiler_params=_cparams(("parallel",)),
    )(*parts, x2, w_out, g, b, rwh, rwl, rb, tri)


def _route(top_idx, rank, tile_cnt, bm, tm):
    t = top_idx.shape[0]
    n_tiles = t // tm
    tile_cnt = tile_cnt[:, :N_EXPERTS]
    tile_base = jnp.cumsum(tile_cnt, axis=0) - tile_cnt
    counts = jnp.sum(tile_cnt, axis=0)
    nblk_e = (counts + bm - 1) // bm
    blk_end = jnp.cumsum(nblk_e)
    blk_start = blk_end - nblk_e
    offs = blk_start[None, :] * bm + tile_base
    onehot = top_idx.reshape(n_tiles, tm, TOP_K, 1) == jnp.arange(N_EXPERTS, dtype=I32)
    dest = rank.reshape(n_tiles, tm, TOP_K) + jnp.sum(jnp.where(onehot, offs[:, None, None, :], 0), axis=-1)
    n_blocks = (t * TOP_K) // bm + N_EXPERTS
    blk = jnp.arange(n_blocks, dtype=I32)
    block_e = jnp.minimum(jnp.sum(blk[:, None] >= blk_end[None, :], axis=1), N_EXPERTS - 1).astype(I32)
    nb = blk_end[-1:].astype(I32)
    cnt = jnp.clip(counts[block_e] - (blk - blk_start[block_e]) * bm, 0, bm)
    cnt = jnp.where(blk < nb[0], cnt, 0).astype(I32)
    return dest.reshape(t * TOP_K).astype(I32), block_e, nb, cnt


def _dispatch_kernel(cnt_ref, dest_ref, h_ref, xs_hbm, stage, zbuf, sem, *, tm, n_steps, bm, n_blocks):
    i = pl.program_id(0)
    slot = i % 2

    @pl.when(i == 0)
    def _():
        zbuf[...] = jnp.zeros_like(zbuf)

        def zero_blocks(act):
            def body(blk, c):
                @pl.when(cnt_ref[blk] < bm)
                def _():
                    act(pltpu.make_async_copy(zbuf, xs_hbm.at[pl.ds(pl.multiple_of(blk * bm, bm), bm)], sem.at[2]))
                return c
            lax.fori_loop(0, n_blocks, body, 0)

        zero_blocks(lambda cp: cp.start())
        zero_blocks(lambda cp: cp.wait())

    stage[slot] = h_ref[...]

    def copy(r, dst, s):
        return pltpu.make_async_copy(stage.at[s, pl.ds(r, 1)], xs_hbm.at[pl.ds(dst, 1)], sem.at[s])

    for r in range(tm):
        for k in range(TOP_K):
            copy(r, dest_ref[0, 0, r * TOP_K + k], slot).start(priority=k % 2)

    def drain(s):
        for _ in range(tm * TOP_K):
            copy(0, 0, s).wait()

    @pl.when(i >= 1)
    def _():
        drain(1 - slot)

    @pl.when(i == n_steps - 1)
    def _():
        drain(slot)


def _dispatch(h, dest, cnt, n_rows):
    t = h.shape[0]
    tm = TM_DISP
    bm = BM_MOE
    n_steps = t // tm
    grid_spec = pltpu.PrefetchScalarGridSpec(
        num_scalar_prefetch=1,
        grid=(n_steps,),
        in_specs=[pl.BlockSpec((1, 1, tm * TOP_K), lambda i, *_: (i, 0, 0), memory_space=pltpu.SMEM),
                  pl.BlockSpec((tm, D_MODEL), lambda i, *_: (i, 0))],
        out_specs=pl.BlockSpec(memory_space=pl.ANY),
        scratch_shapes=[pltpu.VMEM((2, tm, D_MODEL), F32), pltpu.VMEM((bm, D_MODEL), F32),
                        pltpu.SemaphoreType.DMA((3,))])
    return pl.pallas_call(
        functools.partial(_dispatch_kernel, tm=tm, n_steps=n_steps, bm=bm, n_blocks=n_rows // bm),
        grid_spec=grid_spec,
        out_shape=jax.ShapeDtypeStruct((n_rows, D_MODEL), F32),
        compiler_params=_cparams(("arbitrary",)),
    )(cnt, dest.reshape(n_steps, 1, tm * TOP_K), h)


def _expert_kernel(be_ref, nb_ref, x_ref, wgu_ref, bgu_ref, wd_ref, bd_ref, y_ref, wgu_bf, wd_bf):
    i = pl.program_id(0)
    live = i < nb_ref[0]
    first_of_expert = (i == 0) | (be_ref[i] != be_ref[jnp.maximum(i - 1, 0)])

    @pl.when(live & first_of_expert)
    def _():
        wgu_bf[...] = wgu_ref[0].astype(BF16)
        wd_bf[...] = wd_ref[0].astype(BF16)

    @pl.when(live)
    def _():
        x = x_ref[...].astype(BF16)
        acc = jnp.zeros(y_ref.shape, F32)
        for j in range(D_EXPERT // HC_MOE):
            lo = j * HC_MOE
            gate = (jnp.dot(x, wgu_bf[:, lo:lo + HC_MOE], preferred_element_type=F32)
                    + bgu_ref[0, :, lo:lo + HC_MOE])
            up = (jnp.dot(x, wgu_bf[:, D_EXPERT + lo:D_EXPERT + lo + HC_MOE], preferred_element_type=F32)
                  + bgu_ref[0, :, D_EXPERT + lo:D_EXPERT + lo + HC_MOE])
            gate = jnp.minimum(gate, SWIGLU_LIMIT)
            up = jnp.clip(up, -SWIGLU_LIMIT, SWIGLU_LIMIT)
            act = (up + 1.0) * gate * (1.0 / (1.0 + jnp.exp(-SWIGLU_ALPHA * gate)))
            acc = acc + jnp.dot(act.astype(BF16), wd_bf[lo:lo + HC_MOE, :], preferred_element_type=F32)
        y_ref[...] = acc + bd_ref[0]

    @pl.when(jnp.logical_not(live))
    def _():
        y_ref[...] = jnp.zeros_like(y_ref)


def _experts(xs, block_e, nb, wgu, bgu, wd, bd, layer):
    bm = BM_MOE
    n_blocks = xs.shape[0] // bm
    per_expert = lambda *dims: pl.BlockSpec((None, 1) + dims, lambda i, be, *_: (layer, be[i], 0, 0))
    grid_spec = pltpu.PrefetchScalarGridSpec(
        num_scalar_prefetch=2,
        grid=(n_blocks,),
        in_specs=[pl.BlockSpec((bm, D_MODEL), lambda i, *_: (i, 0)),
                  per_expert(D_MODEL, 2 * D_EXPERT), per_expert(1, 2 * D_EXPERT),
                  per_expert(D_EXPERT, D_MODEL), per_expert(1, D_MODEL)],
        out_specs=pl.BlockSpec((bm, D_MODEL), lambda i, *_: (i, 0)),
        scratch_shapes=[pltpu.VMEM((D_MODEL, 2 * D_EXPERT), BF16), pltpu.VMEM((D_EXPERT, D_MODEL), BF16)])
    return pl.pallas_call(
        _expert_kernel,
        grid_spec=grid_spec,
        out_shape=jax.ShapeDtypeStruct(xs.shape, F32),
        compiler_params=_cparams(("arbitrary",)),
    )(block_e, nb, xs, wgu, bgu, wd, bd)


def _combine_kernel(dest_ref, nxt_ref, gate_ref, h_ref, g_ref, b_ref, ys_hbm, o_ref, buf, sem, *, tm, n_steps):
    i = pl.program_id(0)
    slot = i % 2

    def copy(src, r, k, s):
        return pltpu.make_async_copy(ys_hbm.at[pl.ds(src, 1)], buf.at[s, k, pl.ds(r, 1)], sem.at[s])

    def issue(idx_ref, s):
        for r in range(tm):
            for k in range(TOP_K):
                copy(idx_ref[0, 0, r * TOP_K + k], r, k, s).start(priority=k % 2)

    @pl.when(i == 0)
    def _():
        issue(dest_ref, 0)

    @pl.when(i + 1 < n_steps)
    def _():
        issue(nxt_ref, 1 - slot)

    for _ in range(tm * TOP_K):
        copy(0, 0, 0, slot).wait()

    gates = gate_ref[...]
    f = jnp.zeros(o_ref.shape, F32)
    for k in range(TOP_K):
        f = f + gates[:, k:k + 1] * buf[slot, k]
    o_ref[...] = _layer_norm(DEEPNORM_ALPHA * h_ref[...] + f, g_ref[...], b_ref[...])


def _combine(ys, dest, gates, h, g, b):
    t = h.shape[0]
    tm = TM_COMB
    n_steps = t // tm
    row = lambda w: pl.BlockSpec((tm, w), lambda i: (i, 0))
    full = lambda a: pl.BlockSpec(a.shape, lambda i: (0,) * a.ndim)
    idx_spec = lambda f: pl.BlockSpec((1, 1, tm * TOP_K), f, memory_space=pltpu.SMEM)
    dest3 = dest.reshape(n_steps, 1, tm * TOP_K)
    return pl.pallas_call(
        functools.partial(_combine_kernel, tm=tm, n_steps=n_steps),
        grid=(n_steps,),
        in_specs=[idx_spec(lambda i: (i, 0, 0)), idx_spec(lambda i: (jnp.minimum(i + 1, n_steps - 1), 0, 0)),
                  row(LANES), row(D_MODEL), full(g), full(b), pl.BlockSpec(memory_space=pl.ANY)],
        out_specs=row(D_MODEL),
        out_shape=jax.ShapeDtypeStruct((t, D_MODEL), F32),
        scratch_shapes=[pltpu.VMEM((2, TOP_K, tm, D_MODEL), F32), pltpu.SemaphoreType.DMA((2,))],
        compiler_params=_cparams(("arbitrary",)),
    )(dest3, dest3, gates, h, g, b, ys)


def _pad_cols(w, width):
    return jnp.pad(w, ((0, 0), (0, width - w.shape[1])))


def _pack_layer(w_in, gla_w_gate, gla_b_gate, gla_norm, pool_w, pool_scale, diff_lq1, diff_lk1,
                diff_lq2, diff_lk2, diff_norm, mla_q_norm, mla_w_uq, mla_kv_norm, mla_w_ukv,
                router_w, router_b):
    offs = [0, 128, 256, 512, 768, 784, 1040, 1296, 1552, 1808, 2000, 2160]
    seg = lambda i: w_in[:, offs[i]:offs[i + 1]]
    dvh = GROUP_WIDTH // N_HEADS
    spread_heads = lambda w: jnp.concatenate(
        [_pad_cols(w[:, h * dvh:(h + 1) * dvh], LANES) for h in range(N_HEADS)], axis=1)
    w_big = jnp.concatenate(
        [seg(0), seg(1), seg(2), seg(3), _pad_cols(seg(4), LANES), seg(5), seg(6), seg(7), spread_heads(seg(8)),
         _pad_cols(seg(9), 256), _pad_cols(seg(10), 256)], axis=1).astype(BF16)
    wg = jnp.pad(gla_w_gate, ((0, LANES - GLA_GATE_RANK), (0, 0))).astype(BF16)
    bg = gla_b_gate.reshape(1, LANES)
    gnorm = jnp.tile(gla_norm, N_HEADS).reshape(1, 256)
    eye = jnp.eye(N_HEADS, dtype=F32)
    ind = jnp.kron(eye, jnp.ones((GLA_DK, GLA_DV), F32)).astype(BF16)
    tri = jnp.tril(jnp.ones((GLA_CHUNK, GLA_CHUNK), F32)).astype(BF16)
    pool_bd = jax.scipy.linalg.block_diag(*[pool_w[g] for g in range(4)]).astype(BF16)
    pscale = pool_scale.reshape(1, 256)
    aux = jnp.zeros((8, LANES), F32)
    aux = aux.at[0, :DIFF_DH].set(diff_lq1).at[1, :DIFF_DH].set(diff_lk1)
    aux = aux.at[2, :DIFF_DH].set(diff_lq2).at[3, :DIFF_DH].set(diff_lk2)
    aux = aux.at[4, :2 * DIFF_DH].set(diff_norm)
    qn = _pad_cols(mla_q_norm.reshape(1, MLA_Q_RANK), 256)
    kvn = mla_kv_norm.reshape(1, MLA_KV_RANK)
    dqk = MLA_NOPE + MLA_ROPE
    wuq = jnp.concatenate([_pad_cols(mla_w_uq[:, h * dqk:(h + 1) * dqk], LANES) for h in range(N_HEADS)], axis=1)
    wuq = jnp.pad(wuq, ((0, 256 - MLA_Q_RANK), (0, 0))).astype(BF16)
    dkv = MLA_NOPE + MLA_V
    wk = jnp.concatenate([_pad_cols(mla_w_ukv[:, h * dkv:h * dkv + MLA_NOPE], LANES) for h in range(N_HEADS)], axis=1)
    wv = jnp.concatenate([mla_w_ukv[:, h * dkv + MLA_NOPE:(h + 1) * dkv] for h in range(N_HEADS)], axis=1)
    wukv = jnp.concatenate([wk, spread_heads(wv)], axis=1).astype(BF16)
    sel = jnp.zeros((LANES, 4 * LANES), F32)
    for h in range(N_HEADS):
        sel = sel.at[jnp.arange(MLA_ROPE), h * LANES + MLA_NOPE + jnp.arange(MLA_ROPE)].set(1.0)
    sel = sel.astype(BF16)
    rw = _pad_cols(router_w, LANES)
    rwh = rw.astype(BF16)
    rwl = (rw - rwh.astype(F32)).astype(BF16)
    rb = _pad_cols(router_b.reshape(1, N_EXPERTS), LANES)
    return dict(w_big=w_big, wg=wg, bg=bg, gnorm=gnorm, ind=ind, tri=tri, pool_bd=pool_bd, pscale=pscale,
                aux=aux, qn=qn, kvn=kvn, wuq=wuq, wukv=wukv, sel=sel, rwh=rwh, rwl=rwl, rb=rb)


def kernel(x, positions, w_in, gla_w_gate, gla_b_gate, gla_norm, pool_w, pool_scale, diff_lq1, diff_lk1,
           diff_lq2, diff_lk2, diff_norm, mla_q_norm, mla_w_uq, mla_kv_norm, mla_w_ukv, w_out, ln1_g, ln1_b,
           router_w, router_b, w_gate_up, b_gate_up, w_down, b_down, ln2_g, ln2_b):
    batch, seq, d = x.shape
    t = batch * seq
    assert d == D_MODEL and seq % TQ == 0 and t % TM_PROJ == 0 and (t * TOP_K) % BM_MOE == 0

    inv = 1.0 / (ROPE_THETA ** (jnp.arange(0, DIFF_DH, 2, dtype=F32) / DIFF_DH))
    ang = positions.astype(F32).reshape(t, 1) * inv[None, :]
    cos, sin = jnp.cos(ang), jnp.sin(ang)
    cos128 = jnp.tile(jnp.concatenate([cos, cos], axis=1), (1, 4))
    sin128 = jnp.tile(jnp.concatenate([-sin, sin], axis=1), (1, 4))

    tri_out = jnp.tril(jnp.ones((TM_OUT, TM_OUT), F32)).astype(BF16)

    h = x.reshape(t, d)
    for l in range(DEPTH):
        p = _pack_layer(w_in[l], gla_w_gate[l], gla_b_gate[l], gla_norm[l], pool_w[l], pool_scale[l],
                        diff_lq1[l], diff_lk1[l], diff_lq2[l], diff_lk2[l], diff_norm[l], mla_q_norm[l],
                        mla_w_uq[l], mla_kv_norm[l], mla_w_ukv[l], router_w[l], router_b[l])
        gla_in, pool_in, dq, dk, dv, mq, mk, mv = _proj(
            h, p['w_big'], cos128, sin128, p['qn'], p['kvn'], p['wuq'], p['wukv'], p['sel'])
        gla_out = _gla(gla_in, p['wg'], p['bg'], p['gnorm'], p['ind'], p['tri'], batch)
        pool_out = _pool(pool_in, p['pool_bd'], p['pscale'], batch)
        lam_init = 0.8 - 0.6 * math.exp(-0.3 * l)
        diff_out = _flash(dq, dk, dv, p['aux'], batch, DIFF_HEADS_SPEC, lam_init)
        mla_out = _flash(mq, mk, mv, p['aux'], batch, MLA_HEADS_SPEC, None)
        h1, top_idx, gates, rank, tile_cnt = _outproj(
            (gla_out, pool_out, diff_out, mla_out), h, w_out[l].astype(BF16), ln1_g[l].reshape(1, d),
            ln1_b[l].reshape(1, d), p['rwh'], p['rwl'], p['rb'], tri_out)
        dest, block_e, nb, cnt = _route(top_idx[:, :TOP_K], rank[:, :TOP_K], tile_cnt[:, 7, :], BM_MOE, TM_OUT)
        n_rows = t * TOP_K + N_EXPERTS * BM_MOE
        xs = _dispatch(h1, dest, cnt, n_rows)
        ys = _experts(xs, block_e, nb, w_gate_up, b_gate_up.reshape(DEPTH, N_EXPERTS, 1, 2 * D_EXPERT),
                      w_down, b_down.reshape(DEPTH, N_EXPERTS, 1, D_MODEL), l)
        h = _combine(ys, dest, gates, h1, ln2_g[l].reshape(1, d), ln2_b[l].reshape(1, d))
    return h.reshape(batch, seq, d)
```

```python
import functools
import math

import jax
import jax.numpy as jnp
from jax import lax
from jax.experimental import pallas as pl
from jax.experimental.pallas import tpu as pltpu

F32 = jnp.float32
BF16 = jnp.bfloat16
I32 = jnp.int32

D_MODEL = 1024
DEPTH = 2
GROUP_WIDTH = 256
N_HEADS = 4
GLA_DK = 32
GLA_DV = 64
GLA_GATE_RANK = 16
GLA_GATE_NORM = 16.0
POOL_WINDOWS = (2, 4, 8, 16)
POOL_CH = 64
DIFF_DH = 32
MLA_Q_RANK = 192
MLA_KV_RANK = 128
MLA_NOPE = 64
MLA_ROPE = 32
MLA_V = 64
ROPE_THETA = 10000.0
N_EXPERTS = 32
TOP_K = 4
D_EXPERT = 1024
SWIGLU_LIMIT = 7.0
SWIGLU_ALPHA = 1.702
NORM_EPS = 1e-5
DEEPNORM_ALPHA = (2 * DEPTH) ** 0.25

LANES = 128
MXU_WIDTH = 256
VMEM_LIMIT = 56 * 1024 * 1024

C_GLA = 0
C_POOL = 896
C_DQ = 1152
C_DK = 1408
C_DV = 1664
C_MQA = 2176
C_MKVA = 2432
C_END = 2688
GLA_W = C_POOL - C_GLA
V_EXT = 4 * LANES

NEG = -1e30
LOG2E = math.log2(math.e)

TM_PROJ = 512
TG_GLA = 512
GLA_CHUNK = 64
GLA_SUB = 16
TQ = 512
TM_OUT = 512
BM_MOE = 512
HC_MOE = 512
TM_DISP = 128
TM_COMB = 128


def _cparams(sem):
    return pltpu.CompilerParams(dimension_semantics=sem, vmem_limit_bytes=VMEM_LIMIT)


def _lane_iota(shape):
    return lax.broadcasted_iota(I32, shape, len(shape) - 1)


def _row_iota(shape):
    return lax.broadcasted_iota(I32, shape, len(shape) - 2)


def _rope(x, cos, sin_signed):
    w = x.shape[-1]
    from_low = pltpu.roll(x, 16, axis=1)
    from_high = pltpu.roll(x, w - 16, axis=1)
    swapped = jnp.where((_lane_iota(x.shape) % 32) >= 16, from_low, from_high)
    return x * cos + swapped * sin_signed


def _proj_kernel(x_ref, w_ref, cos_ref, sin_ref, qn_ref, kvn_ref, wuq_ref, wukv_ref, sel_ref,
                 gla_ref, pool_ref, dq_ref, dk_ref, dv_ref, mq_ref, mk_ref, mv_ref):
    xb = x_ref[...].astype(BF16)

    def seg(lo, hi):
        return jnp.dot(xb, w_ref[:, lo:hi], preferred_element_type=F32)

    gla_ref[...] = seg(C_GLA, C_POOL)
    pool_ref[...] = seg(C_POOL, C_DQ)

    cos = cos_ref[...]
    sin = sin_ref[...]
    cos2 = jnp.concatenate([cos, cos], axis=1)
    sin2 = jnp.concatenate([sin, sin], axis=1)
    dq_ref[...] = (_rope(seg(C_DQ, C_DK), cos2, sin2) * (LOG2E * DIFF_DH ** -0.5)).astype(BF16)
    dk_ref[...] = _rope(seg(C_DK, C_DV), cos2, sin2).astype(BF16)
    ones_cols = ((_lane_iota((1, V_EXT)) % LANES) >= GROUP_WIDTH // N_HEADS).astype(F32)
    dv_ref[...] = (seg(C_DV, C_MQA) + ones_cols).astype(BF16)

    mqa = seg(C_MQA, C_MKVA)
    ms = jnp.sum(mqa * mqa, axis=1, keepdims=True) * (1.0 / MLA_Q_RANK)
    cq = (mqa * lax.rsqrt(ms + 1e-6) * qn_ref[...]).astype(BF16)
    mq = jnp.dot(cq, wuq_ref[...], preferred_element_type=F32)
    cos4 = jnp.concatenate([cos2, cos2], axis=1)
    sin4 = jnp.concatenate([sin2, sin2], axis=1)
    lane = _lane_iota(mq.shape) % LANES
    is_rope = (lane >= MLA_NOPE) & (lane < MLA_NOPE + MLA_ROPE)
    mq = jnp.where(is_rope, _rope(mq, cos4, sin4), mq)
    mq_ref[...] = (mq * (LOG2E * (MLA_NOPE + MLA_ROPE) ** -0.5)).astype(BF16)

    mkva = seg(C_MKVA, C_END)
    ckv_raw = mkva[:, :LANES]
    msk = jnp.sum(ckv_raw * ckv_raw, axis=1, keepdims=True) * (1.0 / MLA_KV_RANK)
    ckv = (ckv_raw * lax.rsqrt(msk + 1e-6) * kvn_ref[...]).astype(BF16)
    kv = jnp.dot(ckv, wukv_ref[...], preferred_element_type=F32)
    k_rope = _rope(mkva[:, LANES:], cos, sin).astype(BF16)
    placed = jnp.dot(k_rope, sel_ref[...], preferred_element_type=F32)
    mk_ref[...] = (kv[:, :4 * LANES] + placed).astype(BF16)
    mv_ref[...] = (kv[:, 4 * LANES:] + ones_cols).astype(BF16)


def _proj(x2, w_big, cos128, sin128, qn, kvn, wuq, wukv, sel):
    t = x2.shape[0]
    tm = TM_PROJ
    row = lambda w: pl.BlockSpec((tm, w), lambda i: (i, 0))
    full = lambda a: pl.BlockSpec(a.shape, lambda i: (0,) * a.ndim)
    out_shapes = [
        jax.ShapeDtypeStruct((t, GLA_W), F32), jax.ShapeDtypeStruct((t, 256), F32),
        jax.ShapeDtypeStruct((t, 256), BF16), jax.ShapeDtypeStruct((t, 256), BF16),
        jax.ShapeDtypeStruct((t, V_EXT), BF16), jax.ShapeDtypeStruct((t, 512), BF16),
        jax.ShapeDtypeStruct((t, 512), BF16), jax.ShapeDtypeStruct((t, V_EXT), BF16)]
    return pl.pallas_call(
        _proj_kernel,
        grid=(t // tm,),
        in_specs=[row(D_MODEL), full(w_big), row(LANES), row(LANES), full(qn), full(kvn),
                  full(wuq), full(wukv), full(sel)],
        out_specs=[row(s.shape[1]) for s in out_shapes],
        out_shape=out_shapes,
        compiler_params=_cparams(("parallel",)),
    )(x2, w_big, cos128, sin128, qn, kvn, wuq, wukv, sel)


def _split3(x):
    hi = x.astype(BF16)
    r1 = x - hi.astype(F32)
    mid = r1.astype(BF16)
    lo = (r1 - mid.astype(F32)).astype(BF16)
    return hi, mid, lo


def _gla_kernel(blk_ref, wg_ref, bg_ref, norm_ref, ind_ref, tri_ref, out_ref, st_ref, *, n_chunks):
    c_len = GLA_CHUNK

    @pl.when(pl.program_id(1) == 0)
    def _():
        st_ref[...] = jnp.zeros_like(st_ref)

    same_head = (_row_iota((256, LANES)) // GLA_DV) == (_lane_iota((256, LANES)) // GLA_DK)
    ind = ind_ref[...]
    tri = tri_ref[...]
    rows = _row_iota((8, LANES))

    def chunk(c, carry):
        r0 = pl.multiple_of(c * c_len, c_len)
        blk = blk_ref[pl.ds(r0, c_len), :]
        q = blk[:, 0:128] * (GLA_DK ** -0.5)
        k = blk[:, 128:256]
        v = blk[:, 256:512]
        g = blk[:, 512:768]
        gr = blk[:, 768:896]
        logit = jnp.dot(gr.astype(BF16), wg_ref[...], preferred_element_type=F32) + bg_ref[...]
        log_a = (jnp.minimum(logit, 0.0) - jnp.log(1.0 + jnp.exp(-jnp.abs(logit)))) * (1.0 / GLA_GATE_NORM)
        hi, mid, lo = _split3(log_a)
        b = (jnp.dot(tri, hi, preferred_element_type=F32) + jnp.dot(tri, mid, preferred_element_type=F32)
             + jnp.dot(tri, lo, preferred_element_type=F32))

        st = st_ref[...]
        o_subs = []
        for sc in range(c_len // GLA_SUB):
            lo = sc * GLA_SUB
            bs = b[lo - 1:lo, :] if sc else jnp.zeros((1, LANES), F32)
            b_s, q_s, k_s, v_s = (a[lo:lo + GLA_SUB, :] for a in (b, q, k, v))
            qd = (q_s * jnp.exp(b_s - bs)).astype(BF16)
            o_s = lax.dot_general(qd, st.astype(BF16), (((1,), (1,)), ((), ())), preferred_element_type=F32)
            tiles = [o_s[8 * t:8 * (t + 1), :] for t in range(GLA_SUB // 8)]
            e_tiles, owners = [], []
            for j in range(GLA_SUB):
                for t in range(j // 8, GLA_SUB // 8):
                    rs = slice(8 * t, 8 * (t + 1))
                    e = jnp.exp(b_s[rs, :] - b_s[j:j + 1, :]) * (q_s[rs, :] * k_s[j:j + 1, :])
                    if t == j // 8 and j % 8:
                        e = jnp.where(rows >= j % 8, e, 0.0)
                    e_tiles.append(e)
                    owners.append((j, t))
            p_all = jnp.dot(jnp.concatenate(e_tiles, axis=0).astype(BF16), ind, preferred_element_type=F32)
            for n, (j, t) in enumerate(owners):
                tiles[t] = tiles[t] + p_all[8 * n:8 * (n + 1), :] * v_s[j:j + 1, :]
            o_subs.extend(tiles)
            b_end = b_s[GLA_SUB - 1:GLA_SUB, :]
            kd = (k_s * jnp.exp(b_end - b_s)).astype(BF16)
            upd = lax.dot_general(v_s.astype(BF16), kd, (((0,), (0,)), ((), ())), preferred_element_type=F32)
            st = st * jnp.exp(b_end - bs) + jnp.where(same_head, upd, 0.0)
        st_ref[...] = st
        o = jnp.concatenate(o_subs, axis=0)

        o2 = o * o
        head = _lane_iota(o.shape) // GLA_DV
        scale = jnp.zeros_like(o)
        for h in range(N_HEADS):
            ms = jnp.sum(jnp.where(head == h, o2, 0.0), axis=1, keepdims=True) * (1.0 / GLA_DV)
            scale = jnp.where(head == h, lax.rsqrt(ms + 1e-6), scale)
        silu = g / (1.0 + jnp.exp(-g))
        out_ref[pl.ds(r0, c_len), :] = (o * scale * norm_ref[...] * silu).astype(out_ref.dtype)
        return carry

    lax.fori_loop(0, n_chunks, chunk, 0, unroll=2)


def _gla(gla_in, wg, bg, norm, ind, tri, batch):
    t = gla_in.shape[0]
    s = t // batch
    tg = TG_GLA
    nblk = s // tg
    full = lambda a: pl.BlockSpec(a.shape, lambda b, i: (0,) * a.ndim)
    return pl.pallas_call(
        functools.partial(_gla_kernel, n_chunks=tg // GLA_CHUNK),
        grid=(batch, nblk),
        in_specs=[pl.BlockSpec((tg, GLA_W), lambda b, i: (b * nblk + i, 0)),
                  full(wg), full(bg), full(norm), full(ind), full(tri)],
        out_specs=pl.BlockSpec((tg, 256), lambda b, i: (b * nblk + i, 0)),
        out_shape=jax.ShapeDtypeStruct((t, 256), BF16),
        scratch_shapes=[pltpu.VMEM((256, LANES), F32)],
        compiler_params=_cparams(("parallel", "arbitrary")),
    )(gla_in, wg, bg, norm, ind, tri)


def _pool_kernel(u_ref, w_ref, scale_ref, out_ref):
    u = u_ref[...]
    t = _row_iota(u.shape)

    def shifted(x, k):
        return jnp.where(t >= k, pltpu.roll(x, k, axis=0), 0.0)

    sums = []
    acc = u
    for k in (1, 2, 4, 8):
        acc = acc + shifted(acc, k)
        sums.append(acc)
    group = _lane_iota(u.shape) // POOL_CH
    tf = (t + 1).astype(F32)
    mean = jnp.zeros_like(u)
    for gi, w in enumerate(POOL_WINDOWS):
        mean = jnp.where(group == gi, sums[gi] / jnp.minimum(tf, float(w)), mean)
    pooled = (mean - u).astype(BF16)
    y = jnp.dot(pooled, w_ref[...], preferred_element_type=F32) * scale_ref[...]
    out_ref[...] = y.astype(out_ref.dtype)


def _pool(pool_in, w_bd, scale, batch):
    t = pool_in.shape[0]
    s = t // batch
    full = lambda a: pl.BlockSpec(a.shape, lambda b: (0,) * a.ndim)
    return pl.pallas_call(
        _pool_kernel,
        grid=(batch,),
        in_specs=[pl.BlockSpec((s, 256), lambda b: (b, 0)), full(w_bd), full(scale)],
        out_specs=pl.BlockSpec((s, 256), lambda b: (b, 0)),
        out_shape=jax.ShapeDtypeStruct((t, 256), BF16),
        compiler_params=_cparams(("parallel",)),
    )(pool_in, w_bd, scale)


def _flash_kernel(qi_tab, ki_tab, q_ref, k_ref, v_ref, aux_ref, o_ref, qexp, m_s, acc_s,
                  *, heads, diff_lam_init):
    step = pl.program_id(1)
    qi = qi_tab[step]
    ki = ki_tab[step]
    tq = q_ref.shape[0]
    tk = k_ref.shape[0]
    dv = GROUP_WIDTH // N_HEADS

    @pl.when(ki == 0)
    def _():
        for vh, (grp, lo, width, _) in enumerate(heads):
            qg = q_ref[:, grp * LANES:(grp + 1) * LANES]
            lane = _lane_iota(qg.shape)
            qexp[vh] = jnp.where((lane >= lo) & (lane < lo + width), qg, jnp.zeros_like(qg))
        m_s[...] = jnp.full_like(m_s, NEG)
        acc_s[...] = jnp.zeros_like(acc_s)

    def update(masked):
        if masked:
            keep = _lane_iota((tq, tk)) <= _row_iota((tq, tk))
        for vh, (grp, _, _, hv) in enumerate(heads):
            kg = k_ref[:, grp * LANES:(grp + 1) * LANES]
            s = lax.dot_general(qexp[vh], kg, (((1,), (1,)), ((), ())), preferred_element_type=F32)
            if masked:
                s = jnp.where(keep, s, NEG)
            m_old = m_s[vh]
            m_new = jnp.maximum(m_old, jnp.max(s, axis=1, keepdims=True))
            alpha = jnp.exp2(m_old - m_new)
            p = jnp.exp2((s - jnp.concatenate([m_new] * (tk // LANES), axis=1)).astype(BF16))
            pv = jnp.dot(p, v_ref[:, hv * LANES:(hv + 1) * LANES], preferred_element_type=F32)
            acc_s[vh] = alpha * acc_s[vh] + pv
            m_s[vh] = m_new

    @pl.when(ki < qi)
    def _():
        update(False)

    @pl.when(ki == qi)
    def _():
        update(True)
        def normalised(vh):
            a = acc_s[vh]
            return (a / pltpu.roll(a, dv, axis=1))[:, :dv]

        outs = []
        if diff_lam_init is None:
            for vh in range(len(heads)):
                outs.append(normalised(vh))
        else:
            aux = aux_ref[...]
            lam = (jnp.exp(jnp.sum(aux[0:1] * aux[1:2], axis=1, keepdims=True))
                   - jnp.exp(jnp.sum(aux[2:3] * aux[3:4], axis=1, keepdims=True)) + diff_lam_init)
            gain = aux[4:5, :dv]
            for h in range(N_HEADS):
                o = normalised(2 * h) - lam * normalised(2 * h + 1)
                ms = jnp.mean(o * o, axis=1, keepdims=True)
                outs.append(o * lax.rsqrt(ms + 1e-5) * gain * (1.0 - diff_lam_init))
        o_ref[...] = jnp.concatenate(outs, axis=1).astype(o_ref.dtype)


def _flash(q, k, v, aux, batch, heads, diff_lam_init):
    t, wq = q.shape
    s = t // batch
    tq = TQ
    nq = s // tq
    pairs = [(a, b) for a in range(nq) for b in range(a + 1)]
    qi_tab = jnp.asarray([p[0] for p in pairs], I32)
    ki_tab = jnp.asarray([p[1] for p in pairs], I32)
    nv = len(heads)
    dv = GROUP_WIDTH // N_HEADS
    grid_spec = pltpu.PrefetchScalarGridSpec(
        num_scalar_prefetch=2,
        grid=(batch, len(pairs)),
        in_specs=[pl.BlockSpec((tq, wq), lambda b, i, qt, kt: (b * nq + qt[i], 0)),
                  pl.BlockSpec((tq, wq), lambda b, i, qt, kt: (b * nq + kt[i], 0)),
                  pl.BlockSpec((tq, V_EXT), lambda b, i, qt, kt: (b * nq + kt[i], 0)),
                  pl.BlockSpec(aux.shape, lambda b, i, qt, kt: (0, 0))],
        out_specs=pl.BlockSpec((tq, 256), lambda b, i, qt, kt: (b * nq + qt[i], 0)),
        scratch_shapes=[pltpu.VMEM((nv, tq, LANES), BF16), pltpu.VMEM((nv, tq, LANES), F32),
                        pltpu.VMEM((nv, tq, LANES), F32)])
    return pl.pallas_call(
        functools.partial(_flash_kernel, heads=heads, diff_lam_init=diff_lam_init),
        grid_spec=grid_spec,
        out_shape=jax.ShapeDtypeStruct((t, 256), BF16),
        compiler_params=_cparams(("parallel", "arbitrary")),
    )(qi_tab, ki_tab, q, k, v, aux)


DIFF_HEADS_SPEC = tuple(((2 * h + m) // 4, 32 * ((2 * h + m) % 4), 32, h)
                        for h in range(N_HEADS) for m in range(2))
MLA_HEADS_SPEC = tuple((h, 0, LANES, h) for h in range(N_HEADS))


def _layer_norm(z, g, b):
    mu = jnp.mean(z, axis=1, keepdims=True)
    zc = z - mu
    var = jnp.mean(zc * zc, axis=1, keepdims=True)
    return zc * lax.rsqrt(var + NORM_EPS) * g + b


def _outproj_kernel(a_ref, p_ref, c_ref, d_ref, x_ref, w_ref, g_ref, b_ref, rwh_ref, rwl_ref, rb_ref, tri_ref,
                    h_ref, idx_ref, gate_ref, rank_ref, cnt_ref):
    y = jnp.zeros(h_ref.shape, F32)
    for i, part in enumerate((a_ref, p_ref, c_ref, d_ref)):
        y = y + jnp.dot(part[...], w_ref[i * GROUP_WIDTH:(i + 1) * GROUP_WIDTH, :],
                        preferred_element_type=F32)
    h = _layer_norm(DEEPNORM_ALPHA * x_ref[...] + y, g_ref[...], b_ref[...])
    h_ref[...] = h

    hi = h.astype(BF16)
    lo = (h - hi.astype(F32)).astype(BF16)
    logits = (jnp.dot(hi, rwh_ref[...], preferred_element_type=F32)
              + jnp.dot(lo, rwh_ref[...], preferred_element_type=F32)
              + jnp.dot(hi, rwl_ref[...], preferred_element_type=F32) + rb_ref[...])
    lane = _lane_iota(logits.shape)
    logits = jnp.where(lane < N_EXPERTS, logits, NEG)
    idx_out = jnp.zeros(logits.shape, I32)
    chosen = jnp.zeros(logits.shape, F32)
    top, picks = [], []
    for r in range(TOP_K):
        m = jnp.max(logits, axis=1, keepdims=True)
        idx = jnp.min(jnp.where(logits == m, lane, LANES), axis=1, keepdims=True)
        idx_out = jnp.where(lane == r, idx, idx_out)
        pick = lane == idx
        chosen = jnp.where(pick, 1.0, chosen)
        logits = jnp.where(pick, NEG, logits)
        top.append(m)
        picks.append(pick)
    es = [jnp.exp(m - top[0]) for m in top]
    denom = es[0] + es[1] + es[2] + es[3]
    gates = jnp.zeros(logits.shape, F32)
    for r in range(TOP_K):
        gates = jnp.where(lane == r, es[r] / denom, gates)
    idx_ref[...] = idx_out
    gate_ref[...] = gates

    seen = jnp.dot(tri_ref[...], chosen.astype(BF16), preferred_element_type=F32)
    rank = jnp.zeros(logits.shape, F32)
    for r in range(TOP_K):
        before = jnp.sum(jnp.where(picks[r], seen, 0.0), axis=1, keepdims=True) - 1.0
        rank = jnp.where(lane == r, before, rank)
    rank_ref[...] = rank.astype(I32)
    cnt_ref[0] = seen[seen.shape[0] - 8:, :].astype(I32)


def _outproj(parts, x2, w_out, g, b, rwh, rwl, rb, tri):
    t = x2.shape[0]
    tm = TM_OUT
    row = lambda w: pl.BlockSpec((tm, w), lambda i: (i, 0))
    full = lambda a: pl.BlockSpec(a.shape, lambda i: (0,) * a.ndim)
    return pl.pallas_call(
        _outproj_kernel,
        grid=(t // tm,),
        in_specs=[row(256)] * 4 + [row(D_MODEL), full(w_out), full(g), full(b), full(rwh), full(rwl), full(rb),
                                   full(tri)],
        out_specs=[row(D_MODEL), row(LANES), row(LANES), row(LANES),
                   pl.BlockSpec((1, 8, LANES), lambda i: (i, 0, 0))],
        out_shape=[jax.ShapeDtypeStruct((t, D_MODEL), F32), jax.ShapeDtypeStruct((t, LANES), I32),
                   jax.ShapeDtypeStruct((t, LANES), F32), jax.ShapeDtypeStruct((t, LANES), I32),
                   jax.ShapeDtypeStruct((t // tm, 8, LANES), I32)],
        compiler_params=_cparams(("parallel",)),
    )(*parts, x2, w_out, g, b, rwh, rwl, rb, tri)


def _route(top_idx, rank, tile_cnt, bm, tm):
    t = top_idx.shape[0]
    n_tiles = t // tm
    tile_cnt = tile_cnt[:, :N_EXPERTS]
    tile_base = jnp.cumsum(tile_cnt, axis=0) - tile_cnt
    counts = jnp.sum(tile_cnt, axis=0)
    nblk_e = (counts + bm - 1) // bm
    blk_end = jnp.cumsum(nblk_e)
    blk_start = blk_end - nblk_e
    offs = blk_start[None, :] * bm + tile_base
    onehot = top_idx.reshape(n_tiles, tm, TOP_K, 1) == jnp.arange(N_EXPERTS, dtype=I32)
    dest = rank.reshape(n_tiles, tm, TOP_K) + jnp.sum(jnp.where(onehot, offs[:, None, None, :], 0), axis=-1)
    n_blocks = (t * TOP_K) // bm + N_EXPERTS
    blk = jnp.arange(n_blocks, dtype=I32)
    block_e = jnp.minimum(jnp.sum(blk[:, None] >= blk_end[None, :], axis=1), N_EXPERTS - 1).astype(I32)
    nb = blk_end[-1:].astype(I32)
    cnt = jnp.clip(counts[block_e] - (blk - blk_start[block_e]) * bm, 0, bm)
    cnt = jnp.where(blk < nb[0], cnt, 0).astype(I32)
    return dest.reshape(t * TOP_K).astype(I32), block_e, nb, cnt


def _dispatch_kernel(cnt_ref, dest_ref, h_ref, xs_hbm, stage, zbuf, sem, *, tm, n_steps, bm, n_blocks):
    i = pl.program_id(0)

    @pl.when(i == 0)
    def _():
        zbuf[...] = jnp.zeros_like(zbuf)

        def zero_blocks(act):
            def body(blk, c):
                @pl.when(cnt_ref[blk] < bm)
                def _():
                    act(pltpu.make_async_copy(zbuf, xs_hbm.at[pl.ds(pl.multiple_of(blk * bm, bm), bm)], sem.at[2]))
                return c
            lax.fori_loop(0, n_blocks, body, 0)

        zero_blocks(lambda cp: cp.start())
        zero_blocks(lambda cp: cp.wait())

    def copy(r, dst, s):
        return pltpu.make_async_copy(stage.at[s, pl.ds(r, 1)], xs_hbm.at[pl.ds(dst, 1)], sem.at[s])

    def drain(s):
        for _ in range(tm * TOP_K):
            copy(0, 0, s).wait()

    for s in range(2):
        @pl.when(i >= 1)
        def _():
            drain(s)

        stage[s] = h_ref[s * tm:(s + 1) * tm, :]
        for r in range(tm):
            for k in range(TOP_K):
                dst = dest_ref[0, 0, (s * tm + r) * TOP_K + k]
                copy(r, dst, s).start(priority=k % 2)

    @pl.when(i == n_steps - 1)
    def _():
        drain(0)
        drain(1)


def _dispatch(h, dest, cnt, n_rows):
    t = h.shape[0]
    tm = TM_DISP
    bm = BM_MOE
    n_steps = t // (2 * tm)
    grid_spec = pltpu.PrefetchScalarGridSpec(
        num_scalar_prefetch=1,
        grid=(n_steps,),
        in_specs=[pl.BlockSpec((1, 1, 2 * tm * TOP_K), lambda i, *_: (i, 0, 0), memory_space=pltpu.SMEM),
                  pl.BlockSpec((2 * tm, D_MODEL), lambda i, *_: (i, 0))],
        out_specs=pl.BlockSpec(memory_space=pl.ANY),
        scratch_shapes=[pltpu.VMEM((2, tm, D_MODEL), F32), pltpu.VMEM((bm, D_MODEL), F32),
                        pltpu.SemaphoreType.DMA((3,))])
    return pl.pallas_call(
        functools.partial(_dispatch_kernel, tm=tm, n_steps=n_steps, bm=bm, n_blocks=n_rows // bm),
        grid_spec=grid_spec,
        out_shape=jax.ShapeDtypeStruct((n_rows, D_MODEL), F32),
        compiler_params=_cparams(("arbitrary",)),
    )(cnt, dest.reshape(n_steps, 1, 2 * tm * TOP_K), h)


def _expert_kernel(be_ref, nb_ref, x_ref, wgu_ref, bgu_ref, wd_ref, bd_ref, y_ref, wgu_bf, wd_bf):
    i = pl.program_id(0)
    live = i < nb_ref[0]
    first_of_expert = (i == 0) | (be_ref[i] != be_ref[jnp.maximum(i - 1, 0)])

    @pl.when(live & first_of_expert)
    def _():
        wgu_bf[...] = wgu_ref[0].astype(BF16)
        wd_bf[...] = wd_ref[0].astype(BF16)

    @pl.when(live)
    def _():
        x = x_ref[...].astype(BF16)
        acc = jnp.zeros((x.shape[0], D_MODEL), F32)
        for j in range(D_EXPERT // HC_MOE):
            lo = j * HC_MOE
            gate = (jnp.dot(x, wgu_bf[:, lo:lo + HC_MOE], preferred_element_type=F32)
                    + bgu_ref[0, :, lo:lo + HC_MOE])
            up = (jnp.dot(x, wgu_bf[:, D_EXPERT + lo:D_EXPERT + lo + HC_MOE], preferred_element_type=F32)
                  + bgu_ref[0, :, D_EXPERT + lo:D_EXPERT + lo + HC_MOE])
            gate = jnp.minimum(gate, SWIGLU_LIMIT)
            up = jnp.clip(up, -SWIGLU_LIMIT, SWIGLU_LIMIT)
            act = (up + 1.0) * gate * (1.0 / (1.0 + jnp.exp(-SWIGLU_ALPHA * gate)))
            acc = acc + jnp.dot(act.astype(BF16), wd_bf[lo:lo + HC_MOE, :], preferred_element_type=F32)
        y_ref[...] = acc + bd_ref[0]

    @pl.when(jnp.logical_not(live))
    def _():
        y_ref[...] = jnp.zeros_like(y_ref)


def _experts(xs, block_e, nb, wgu, bgu, wd, bd, layer):
    bm = BM_MOE
    n_blocks = xs.shape[0] // bm
    per_expert = lambda *dims: pl.BlockSpec((None, 1) + dims, lambda i, be, *_: (layer, be[i], 0, 0))
    grid_spec = pltpu.PrefetchScalarGridSpec(
        num_scalar_prefetch=2,
        grid=(n_blocks,),
        in_specs=[pl.BlockSpec((bm, D_MODEL), lambda i, *_: (i, 0)),
                  per_expert(D_MODEL, 2 * D_EXPERT), per_expert(1, 2 * D_EXPERT),
                  per_expert(D_EXPERT, D_MODEL), per_expert(1, D_MODEL)],
        out_specs=pl.BlockSpec((bm, D_MODEL), lambda i, *_: (i, 0)),
        scratch_shapes=[pltpu.VMEM((D_MODEL, 2 * D_EXPERT), BF16), pltpu.VMEM((D_EXPERT, D_MODEL), BF16)])
    return pl.pallas_call(
        _expert_kernel,
        grid_spec=grid_spec,
        out_shape=jax.ShapeDtypeStruct(xs.shape, F32),
        compiler_params=_cparams(("arbitrary",)),
    )(block_e, nb, xs, wgu, bgu, wd, bd)


def _combine_kernel(dest_ref, nxt_ref, gate_ref, h_ref, g_ref, b_ref, ys_hbm, o_ref, buf, sem, *, tm, n_steps):
    i = pl.program_id(0)

    def copy(src, r, k, s):
        return pltpu.make_async_copy(ys_hbm.at[pl.ds(src, 1)], buf.at[s, k, pl.ds(r, 1)], sem.at[s])

    def issue(idx_ref, half, s):
        for r in range(tm):
            for k in range(TOP_K):
                src = idx_ref[0, 0, (half * tm + r) * TOP_K + k]
                copy(src, r, k, s).start(priority=k % 2)

    def finish(half, s):
        for _ in range(tm * TOP_K):
            copy(0, 0, 0, s).wait()
        rows = slice(half * tm, (half + 1) * tm)
        gates = gate_ref[rows, :]
        f = jnp.zeros((tm, D_MODEL), F32)
        for k in range(TOP_K):
            f = f + gates[:, k:k + 1] * buf[s, k]
        o_ref[rows, :] = _layer_norm(DEEPNORM_ALPHA * h_ref[rows, :] + f, g_ref[...], b_ref[...])

    @pl.when(i == 0)
    def _():
        issue(dest_ref, 0, 0)

    issue(dest_ref, 1, 1)
    finish(0, 0)

    @pl.when(i + 1 < n_steps)
    def _():
        issue(nxt_ref, 0, 0)

    finish(1, 1)


def _combine(ys, dest, gates, h, g, b):
    t = h.shape[0]
    tm = TM_COMB
    n_steps = t // (2 * tm)
    row = lambda w: pl.BlockSpec((2 * tm, w), lambda i: (i, 0))
    full = lambda a: pl.BlockSpec(a.shape, lambda i: (0,) * a.ndim)
    idx_spec = lambda f: pl.BlockSpec((1, 1, 2 * tm * TOP_K), f, memory_space=pltpu.SMEM)
    dest3 = dest.reshape(n_steps, 1, 2 * tm * TOP_K)
    return pl.pallas_call(
        functools.partial(_combine_kernel, tm=tm, n_steps=n_steps),
        grid=(n_steps,),
        in_specs=[idx_spec(lambda i: (i, 0, 0)), idx_spec(lambda i: (jnp.minimum(i + 1, n_steps - 1), 0, 0)),
                  row(LANES), row(D_MODEL), full(g), full(b), pl.BlockSpec(memory_space=pl.ANY)],
        out_specs=row(D_MODEL),
        out_shape=jax.ShapeDtypeStruct((t, D_MODEL), F32),
        scratch_shapes=[pltpu.VMEM((2, TOP_K, tm, D_MODEL), F32), pltpu.SemaphoreType.DMA((2,))],
        compiler_params=_cparams(("arbitrary",)),
    )(dest3, dest3, gates, h, g, b, ys)


def _pad_cols(w, width):
    return jnp.pad(w, ((0, 0), (0, width - w.shape[1])))


def _pack_layer(w_in, gla_w_gate, gla_b_gate, gla_norm, pool_w, pool_scale, diff_lq1, diff_lk1,
                diff_lq2, diff_lk2, diff_norm, mla_q_norm, mla_w_uq, mla_kv_norm, mla_w_ukv,
                router_w, router_b):
    offs = [0, 128, 256, 512, 768, 784, 1040, 1296, 1552, 1808, 2000, 2160]
    seg = lambda i: w_in[:, offs[i]:offs[i + 1]]
    dvh = GROUP_WIDTH // N_HEADS
    spread_heads = lambda w: jnp.concatenate(
        [_pad_cols(w[:, h * dvh:(h + 1) * dvh], LANES) for h in range(N_HEADS)], axis=1)
    w_big = jnp.concatenate(
        [seg(0), seg(1), seg(2), seg(3), _pad_cols(seg(4), LANES), seg(5), seg(6), seg(7), spread_heads(seg(8)),
         _pad_cols(seg(9), 256), _pad_cols(seg(10), 256)], axis=1).astype(BF16)
    wg = jnp.pad(gla_w_gate, ((0, LANES - GLA_GATE_RANK), (0, 0))).astype(BF16)
    bg = gla_b_gate.reshape(1, LANES)
    gnorm = jnp.tile(gla_norm, N_HEADS).reshape(1, 256)
    eye = jnp.eye(N_HEADS, dtype=F32)
    ind = jnp.kron(eye, jnp.ones((GLA_DK, GLA_DV), F32)).astype(BF16)
    tri = jnp.tril(jnp.ones((GLA_CHUNK, GLA_CHUNK), F32)).astype(BF16)
    pool_bd = jax.scipy.linalg.block_diag(*[pool_w[g] for g in range(4)]).astype(BF16)
    pscale = pool_scale.reshape(1, 256)
    aux = jnp.zeros((8, LANES), F32)
    aux = aux.at[0, :DIFF_DH].set(diff_lq1).at[1, :DIFF_DH].set(diff_lk1)
    aux = aux.at[2, :DIFF_DH].set(diff_lq2).at[3, :DIFF_DH].set(diff_lk2)
    aux = aux.at[4, :2 * DIFF_DH].set(diff_norm)
    qn = _pad_cols(mla_q_norm.reshape(1, MLA_Q_RANK), 256)
    kvn = mla_kv_norm.reshape(1, MLA_KV_RANK)
    dqk = MLA_NOPE + MLA_ROPE
    wuq = jnp.concatenate([_pad_cols(mla_w_uq[:, h * dqk:(h + 1) * dqk], LANES) for h in range(N_HEADS)], axis=1)
    wuq = jnp.pad(wuq, ((0, 256 - MLA_Q_RANK), (0, 0))).astype(BF16)
    dkv = MLA_NOPE + MLA_V
    wk = jnp.concatenate([_pad_cols(mla_w_ukv[:, h * dkv:h * dkv + MLA_NOPE], LANES) for h in range(N_HEADS)], axis=1)
    wv = jnp.concatenate([mla_w_ukv[:, h * dkv + MLA_NOPE:(h + 1) * dkv] for h in range(N_HEADS)], axis=1)
    wukv = jnp.concatenate([wk, spread_heads(wv)], axis=1).astype(BF16)
    sel = jnp.zeros((LANES, 4 * LANES), F32)
    for h in range(N_HEADS):
        sel = sel.at[jnp.arange(MLA_ROPE), h * LANES + MLA_NOPE + jnp.arange(MLA_ROPE)].set(1.0)
    sel = sel.astype(BF16)
    rw = _pad_cols(router_w, LANES)
    rwh = rw.astype(BF16)
    rwl = (rw - rwh.astype(F32)).astype(BF16)
    rb = _pad_cols(router_b.reshape(1, N_EXPERTS), LANES)
    return dict(w_big=w_big, wg=wg, bg=bg, gnorm=gnorm, ind=ind, tri=tri, pool_bd=pool_bd, pscale=pscale,
                aux=aux, qn=qn, kvn=kvn, wuq=wuq, wukv=wukv, sel=sel, rwh=rwh, rwl=rwl, rb=rb)


def kernel(x, positions, w_in, gla_w_gate, gla_b_gate, gla_norm, pool_w, pool_scale, diff_lq1, diff_lk1,
           diff_lq2, diff_lk2, diff_norm, mla_q_norm, mla_w_uq, mla_kv_norm, mla_w_ukv, w_out, ln1_g, ln1_b,
           router_w, router_b, w_gate_up, b_gate_up, w_down, b_down, ln2_g, ln2_b):
    batch, seq, d = x.shape
    t = batch * seq
    assert d == D_MODEL and seq % TQ == 0 and t % TM_PROJ == 0 and (t * TOP_K) % BM_MOE == 0

    inv = 1.0 / (ROPE_THETA ** (jnp.arange(0, DIFF_DH, 2, dtype=F32) / DIFF_DH))
    ang = positions.astype(F32).reshape(t, 1) * inv[None, :]
    cos, sin = jnp.cos(ang), jnp.sin(ang)
    cos128 = jnp.tile(jnp.concatenate([cos, cos], axis=1), (1, 4))
    sin128 = jnp.tile(jnp.concatenate([-sin, sin], axis=1), (1, 4))

    tri_out = jnp.tril(jnp.ones((TM_OUT, TM_OUT), F32)).astype(BF16)

    h = x.reshape(t, d)
    for l in range(DEPTH):
        p = _pack_layer(w_in[l], gla_w_gate[l], gla_b_gate[l], gla_norm[l], pool_w[l], pool_scale[l],
                        diff_lq1[l], diff_lk1[l], diff_lq2[l], diff_lk2[l], diff_norm[l], mla_q_norm[l],
                        mla_w_uq[l], mla_kv_norm[l], mla_w_ukv[l], router_w[l], router_b[l])
        gla_in, pool_in, dq, dk, dv, mq, mk, mv = _proj(
            h, p['w_big'], cos128, sin128, p['qn'], p['kvn'], p['wuq'], p['wukv'], p['sel'])
        gla_out = _gla(gla_in, p['wg'], p['bg'], p['gnorm'], p['ind'], p['tri'], batch)
        pool_out = _pool(pool_in, p['pool_bd'], p['pscale'], batch)
        lam_init = 0.8 - 0.6 * math.exp(-0.3 * l)
        diff_out = _flash(dq, dk, dv, p['aux'], batch, DIFF_HEADS_SPEC, lam_init)
        mla_out = _flash(mq, mk, mv, p['aux'], batch, MLA_HEADS_SPEC, None)
        h1, top_idx, gates, rank, tile_cnt = _outproj(
            (gla_out, pool_out, diff_out, mla_out), h, w_out[l].astype(BF16), ln1_g[l].reshape(1, d),
            ln1_b[l].reshape(1, d), p['rwh'], p['rwl'], p['rb'], tri_out)
        dest, block_e, nb, cnt = _route(top_idx[:, :TOP_K], rank[:, :TOP_K], tile_cnt[:, 7, :], BM_MOE, TM_OUT)
        n_rows = t * TOP_K + N_EXPERTS * BM_MOE
        xs = _dispatch(h1, dest, cnt, n_rows)
        ys = _experts(xs, block_e, nb, w_gate_up, b_gate_up.reshape(DEPTH, N_EXPERTS, 1, 2 * D_EXPERT),
                      w_down, b_down.reshape(DEPTH, N_EXPERTS, 1, D_MODEL), l)
        h = _combine(ys, dest, gates, h1, ln2_g[l].reshape(1, d), ln2_b[l].reshape(1, d))
    return h.reshape(batch, seq, d)
```

```python
import functools
import math

import jax
import jax.numpy as jnp
from jax import lax
from jax.experimental import pallas as pl
from jax.experimental.pallas import tpu as pltpu

F32 = jnp.float32
BF16 = jnp.bfloat16
I32 = jnp.int32

D_MODEL = 1024
DEPTH = 2
GROUP_WIDTH = 256
N_HEADS = 4
GLA_DK = 32
GLA_DV = 64
GLA_GATE_RANK = 16
GLA_GATE_NORM = 16.0
POOL_WINDOWS = (2, 4, 8, 16)
POOL_CH = 64
DIFF_DH = 32
MLA_Q_RANK = 192
MLA_KV_RANK = 128
MLA_NOPE = 64
MLA_ROPE = 32
MLA_V = 64
ROPE_THETA = 10000.0
N_EXPERTS = 32
TOP_K = 4
D_EXPERT = 1024
SWIGLU_LIMIT = 7.0
SWIGLU_ALPHA = 1.702
NORM_EPS = 1e-5
DEEPNORM_ALPHA = (2 * DEPTH) ** 0.25

LANES = 128
MXU_WIDTH = 256
VMEM_LIMIT = 56 * 1024 * 1024

C_GLA = 0
C_POOL = 896
C_DQ = 1152
C_DK = 1408
C_DV = 1664
C_MQA = 2176
C_MKVA = 2432
C_END = 2688
GLA_W = C_POOL - C_GLA
V_EXT = 4 * LANES

NEG = -1e30
LOG2E = math.log2(math.e)

TM_PROJ = 512
TG_GLA = 512
GLA_CHUNK = 64
GLA_SUB = 16
TQ = 512
TM_OUT = 512
BM_MOE = 512
HC_MOE = 512
TM_DISP = 128
TM_COMB = 128


def _cparams(sem):
    return pltpu.CompilerParams(dimension_semantics=sem, vmem_limit_bytes=VMEM_LIMIT)


def _lane_iota(shape):
    return lax.broadcasted_iota(I32, shape, len(shape) - 1)


def _row_iota(shape):
    return lax.broadcasted_iota(I32, shape, len(shape) - 2)


def _rope(x, cos, sin_signed):
    w = x.shape[-1]
    from_low = pltpu.roll(x, 16, axis=1)
    from_high = pltpu.roll(x, w - 16, axis=1)
    swapped = jnp.where((_lane_iota(x.shape) % 32) >= 16, from_low, from_high)
    return x * cos + swapped * sin_signed


def _proj_kernel(x_ref, w_ref, cos_ref, sin_ref, qn_ref, kvn_ref, wuq_ref, wukv_ref, sel_ref,
                 gla_ref, pool_ref, dq_ref, dk_ref, dv_ref, mq_ref, mk_ref, mv_ref):
    xb = x_ref[...].astype(BF16)

    def seg(lo, hi):
        return jnp.dot(xb, w_ref[:, lo:hi], preferred_element_type=F32)

    gla_ref[...] = seg(C_GLA, C_POOL)
    pool_ref[...] = seg(C_POOL, C_DQ)

    cos = cos_ref[...]
    sin = sin_ref[...]
    cos2 = jnp.concatenate([cos, cos], axis=1)
    sin2 = jnp.concatenate([sin, sin], axis=1)
    dq_ref[...] = (_rope(seg(C_DQ, C_DK), cos2, sin2) * (LOG2E * DIFF_DH ** -0.5)).astype(BF16)
    dk_ref[...] = _rope(seg(C_DK, C_DV), cos2, sin2).astype(BF16)
    ones_cols = ((_lane_iota((1, V_EXT)) % LANES) >= GROUP_WIDTH // N_HEADS).astype(F32)
    dv_ref[...] = (seg(C_DV, C_MQA) + ones_cols).astype(BF16)

    mqa = seg(C_MQA, C_MKVA)
    ms = jnp.sum(mqa * mqa, axis=1, keepdims=True) * (1.0 / MLA_Q_RANK)
    cq = (mqa * lax.rsqrt(ms + 1e-6) * qn_ref[...]).astype(BF16)
    mq = jnp.dot(cq, wuq_ref[...], preferred_element_type=F32)
    cos4 = jnp.concatenate([cos2, cos2], axis=1)
    sin4 = jnp.concatenate([sin2, sin2], axis=1)
    lane = _lane_iota(mq.shape) % LANES
    is_rope = (lane >= MLA_NOPE) & (lane < MLA_NOPE + MLA_ROPE)
    mq = jnp.where(is_rope, _rope(mq, cos4, sin4), mq)
    mq_ref[...] = (mq * (LOG2E * (MLA_NOPE + MLA_ROPE) ** -0.5)).astype(BF16)

    mkva = seg(C_MKVA, C_END)
    ckv_raw = mkva[:, :LANES]
    msk = jnp.sum(ckv_raw * ckv_raw, axis=1, keepdims=True) * (1.0 / MLA_KV_RANK)
    ckv = (ckv_raw * lax.rsqrt(msk + 1e-6) * kvn_ref[...]).astype(BF16)
    kv = jnp.dot(ckv, wukv_ref[...], preferred_element_type=F32)
    k_rope = _rope(mkva[:, LANES:], cos, sin).astype(BF16)
    placed = jnp.dot(k_rope, sel_ref[...], preferred_element_type=F32)
    mk_ref[...] = (kv[:, :4 * LANES] + placed).astype(BF16)
    mv_ref[...] = (kv[:, 4 * LANES:] + ones_cols).astype(BF16)


def _proj(x2, w_big, cos128, sin128, qn, kvn, wuq, wukv, sel):
    t = x2.shape[0]
    tm = TM_PROJ
    row = lambda w: pl.BlockSpec((tm, w), lambda i: (i, 0))
    full = lambda a: pl.BlockSpec(a.shape, lambda i: (0,) * a.ndim)
    out_shapes = [
        jax.ShapeDtypeStruct((t, GLA_W), F32), jax.ShapeDtypeStruct((t, 256), F32),
        jax.ShapeDtypeStruct((t, 256), BF16), jax.ShapeDtypeStruct((t, 256), BF16),
        jax.ShapeDtypeStruct((t, V_EXT), BF16), jax.ShapeDtypeStruct((t, 512), BF16),
        jax.ShapeDtypeStruct((t, 512), BF16), jax.ShapeDtypeStruct((t, V_EXT), BF16)]
    return pl.pallas_call(
        _proj_kernel,
        grid=(t // tm,),
        in_specs=[row(D_MODEL), full(w_big), row(LANES), row(LANES), full(qn), full(kvn),
                  full(wuq), full(wukv), full(sel)],
        out_specs=[row(s.shape[1]) for s in out_shapes],
        out_shape=out_shapes,
        compiler_params=_cparams(("parallel",)),
    )(x2, w_big, cos128, sin128, qn, kvn, wuq, wukv, sel)


def _split3(x):
    hi = x.astype(BF16)
    r1 = x - hi.astype(F32)
    mid = r1.astype(BF16)
    lo = (r1 - mid.astype(F32)).astype(BF16)
    return hi, mid, lo


def _gla_kernel(blk_ref, wg_ref, bg_ref, norm_ref, ind_ref, tri_ref, out_ref, st_ref, *, n_chunks):
    c_len = GLA_CHUNK

    @pl.when(pl.program_id(1) == 0)
    def _():
        st_ref[...] = jnp.zeros_like(st_ref)

    same_head = (_row_iota((256, LANES)) // GLA_DV) == (_lane_iota((256, LANES)) // GLA_DK)
    ind = ind_ref[...]
    tri = tri_ref[...]
    rows = _row_iota((8, LANES))

    n_sub = c_len // GLA_SUB
    same_head_kv = (_row_iota((LANES, 256)) // GLA_DK) == (_lane_iota((LANES, 256)) // GLA_DV)

    def chunk(c, carry):
        r0 = pl.multiple_of(c * c_len, c_len)
        blk = blk_ref[pl.ds(r0, c_len), :]
        q = blk[:, 0:128] * (GLA_DK ** -0.5)
        k = blk[:, 128:256]
        v = blk[:, 256:512]
        g = blk[:, 512:768]
        gr = blk[:, 768:896]
        logit = jnp.dot(gr.astype(BF16), wg_ref[...], preferred_element_type=F32) + bg_ref[...]
        log_a = (jnp.minimum(logit, 0.0) - jnp.log(1.0 + jnp.exp(-jnp.abs(logit)))) * (1.0 / GLA_GATE_NORM)
        parts = jnp.concatenate(_split3(log_a), axis=1)
        b3 = jnp.dot(tri, parts, preferred_element_type=F32)
        b = b3[:, :LANES] + b3[:, LANES:2 * LANES] + b3[:, 2 * LANES:]

        st = st_ref[...]
        vb = v.astype(BF16)
        o0 = lax.dot_general((q * jnp.exp(b)).astype(BF16), st.astype(BF16), (((1,), (1,)), ((), ())),
                             preferred_element_type=F32)
        ends = [b[(u + 1) * GLA_SUB - 1:(u + 1) * GLA_SUB, :] for u in range(n_sub)]
        zero_sub = jnp.zeros((GLA_SUB, LANES), BF16)
        kd_rows = []
        for u in range(n_sub - 1):
            rs = slice(u * GLA_SUB, (u + 1) * GLA_SUB)
            kd_u = (k[rs, :] * jnp.exp(ends[u] - b[rs, :])).astype(BF16)
            kd_rows.append(jnp.concatenate([kd_u if w == u else zero_sub for w in range(n_sub - 1)], axis=1))
        kd_all = jnp.concatenate(kd_rows, axis=0)
        n_prev = (n_sub - 1) * GLA_SUB
        u_all = lax.dot_general(kd_all, vb[:n_prev, :], (((0,), (0,)), ((), ())), preferred_element_type=F32)
        u_all = jnp.concatenate(
            [jnp.where(same_head_kv, u_all[u * LANES:(u + 1) * LANES, :], 0.0) for u in range(n_sub - 1)],
            axis=0).astype(BF16)

        o_subs = []
        for sc in range(n_sub):
            lo = sc * GLA_SUB
            b_s, q_s, k_s, v_s = (a[lo:lo + GLA_SUB, :] for a in (b, q, k, v))
            e_tiles, owners = [], []
            for j in range(GLA_SUB):
                for t in range(j // 8, GLA_SUB // 8):
                    rs = slice(8 * t, 8 * (t + 1))
                    e = jnp.exp(b_s[rs, :] - b_s[j:j + 1, :]) * (q_s[rs, :] * k_s[j:j + 1, :])
                    if t == j // 8 and j % 8:
                        e = jnp.where(rows >= j % 8, e, 0.0)
                    e_tiles.append(e)
                    owners.append((j, t))
            e_all = jnp.concatenate(e_tiles, axis=0).astype(BF16)
            if sc == 0:
                p_all = jnp.dot(e_all, ind, preferred_element_type=F32)
                prev = None
            else:
                q_prev = jnp.concatenate([(q_s * jnp.exp(b_s - ends[u])).astype(BF16) for u in range(sc)], axis=1)
                top = jnp.concatenate([q_prev, zero_sub], axis=1)
                bottom = jnp.concatenate([jnp.zeros((e_all.shape[0], sc * LANES), BF16), e_all], axis=1)
                rhs = jnp.concatenate([u_all[:sc * LANES, :], ind], axis=0)
                both = jnp.dot(jnp.concatenate([top, bottom], axis=0), rhs, preferred_element_type=F32)
                prev = both[:GLA_SUB, :]
                p_all = both[GLA_SUB:, :]
            tiles = []
            for t in range(GLA_SUB // 8):
                rs = slice(8 * t, 8 * (t + 1))
                tiles.append(o0[lo + 8 * t:lo + 8 * (t + 1), :] + (prev[rs, :] if sc else 0.0))
            for n, (j, t) in enumerate(owners):
                tiles[t] = tiles[t] + p_all[8 * n:8 * (n + 1), :] * v_s[j:j + 1, :]
            o_subs.extend(tiles)
        o = jnp.concatenate(o_subs, axis=0)

        b_last = ends[n_sub - 1]
        kd_last = (k * jnp.exp(b_last - b)).astype(BF16)
        upd = lax.dot_general(vb, kd_last, (((0,), (0,)), ((), ())), preferred_element_type=F32)
        st_ref[...] = st * jnp.exp(b_last) + jnp.where(same_head, upd, 0.0)

        o2 = o * o
        head = _lane_iota(o.shape) // GLA_DV
        scale = jnp.zeros_like(o)
        for h in range(N_HEADS):
            ms = jnp.sum(jnp.where(head == h, o2, 0.0), axis=1, keepdims=True) * (1.0 / GLA_DV)
            scale = jnp.where(head == h, lax.rsqrt(ms + 1e-6), scale)
        silu = g / (1.0 + jnp.exp(-g))
        out_ref[pl.ds(r0, c_len), :] = (o * scale * norm_ref[...] * silu).astype(out_ref.dtype)
        return carry

    lax.fori_loop(0, n_chunks, chunk, 0, unroll=2)


def _gla(gla_in, wg, bg, norm, ind, tri, batch):
    t = gla_in.shape[0]
    s = t // batch
    tg = TG_GLA
    nblk = s // tg
    full = lambda a: pl.BlockSpec(a.shape, lambda b, i: (0,) * a.ndim)
    return pl.pallas_call(
        functools.partial(_gla_kernel, n_chunks=tg // GLA_CHUNK),
        grid=(batch, nblk),
        in_specs=[pl.BlockSpec((tg, GLA_W), lambda b, i: (b * nblk + i, 0)),
                  full(wg), full(bg), full(norm), full(ind), full(tri)],
        out_specs=pl.BlockSpec((tg, 256), lambda b, i: (b * nblk + i, 0)),
        out_shape=jax.ShapeDtypeStruct((t, 256), BF16),
        scratch_shapes=[pltpu.VMEM((256, LANES), F32)],
        compiler_params=_cparams(("parallel", "arbitrary")),
    )(gla_in, wg, bg, norm, ind, tri)


def _pool_kernel(u_ref, w_ref, scale_ref, out_ref):
    u = u_ref[...]
    t = _row_iota(u.shape)

    def shifted(x, k):
        return jnp.where(t >= k, pltpu.roll(x, k, axis=0), 0.0)

    sums = []
    acc = u
    for k in (1, 2, 4, 8):
        acc = acc + shifted(acc, k)
        sums.append(acc)
    group = _lane_iota(u.shape) // POOL_CH
    tf = (t + 1).astype(F32)
    mean = jnp.zeros_like(u)
    for gi, w in enumerate(POOL_WINDOWS):
        mean = jnp.where(group == gi, sums[gi] / jnp.minimum(tf, float(w)), mean)
    pooled = (mean - u).astype(BF16)
    y = jnp.dot(pooled, w_ref[...], preferred_element_type=F32) * scale_ref[...]
    out_ref[...] = y.astype(out_ref.dtype)


def _pool(pool_in, w_bd, scale, batch):
    t = pool_in.shape[0]
    s = t // batch
    full = lambda a: pl.BlockSpec(a.shape, lambda b: (0,) * a.ndim)
    return pl.pallas_call(
        _pool_kernel,
        grid=(batch,),
        in_specs=[pl.BlockSpec((s, 256), lambda b: (b, 0)), full(w_bd), full(scale)],
        out_specs=pl.BlockSpec((s, 256), lambda b: (b, 0)),
        out_shape=jax.ShapeDtypeStruct((t, 256), BF16),
        compiler_params=_cparams(("parallel",)),
    )(pool_in, w_bd, scale)


def _flash_kernel(qi_tab, ki_tab, q_ref, k_ref, v_ref, aux_ref, o_ref, qexp, m_s, acc_s,
                  *, heads, diff_lam_init):
    step = pl.program_id(1)
    qi = qi_tab[step]
    ki = ki_tab[step]
    tq = q_ref.shape[0]
    tk = k_ref.shape[0]
    dv = GROUP_WIDTH // N_HEADS

    @pl.when(ki == 0)
    def _():
        for vh, (grp, lo, width, _) in enumerate(heads):
            qg = q_ref[:, grp * LANES:(grp + 1) * LANES]
            lane = _lane_iota(qg.shape)
            qexp[vh] = jnp.where((lane >= lo) & (lane < lo + width), qg, jnp.zeros_like(qg))
        m_s[...] = jnp.full_like(m_s, NEG)
        acc_s[...] = jnp.zeros_like(acc_s)

    def update(masked):
        if masked:
            keep = _lane_iota((tq, tk)) <= _row_iota((tq, tk))
        for vh, (grp, _, _, hv) in enumerate(heads):
            kg = k_ref[:, grp * LANES:(grp + 1) * LANES]
            s = lax.dot_general(qexp[vh], kg, (((1,), (1,)), ((), ())), preferred_element_type=F32)
            if masked:
                s = jnp.where(keep, s, NEG)
            m_old = m_s[vh]
            m_new = jnp.maximum(m_old, jnp.max(s, axis=1, keepdims=True))
            alpha = jnp.exp2(m_old - m_new)
            p = jnp.exp2((s - jnp.concatenate([m_new] * (tk // LANES), axis=1)).astype(BF16))
            pv = jnp.dot(p, v_ref[:, hv * LANES:(hv + 1) * LANES], preferred_element_type=F32)
            acc_s[vh] = alpha * acc_s[vh] + pv
            m_s[vh] = m_new

    @pl.when(ki < qi)
    def _():
        update(False)

    @pl.when(ki == qi)
    def _():
        update(True)
        def normalised(vh):
            a = acc_s[vh]
            return (a / pltpu.roll(a, dv, axis=1))[:, :dv]

        outs = []
        if diff_lam_init is None:
            for vh in range(len(heads)):
                outs.append(normalised(vh))
        else:
            aux = aux_ref[...]
            lam = (jnp.exp(jnp.sum(aux[0:1] * aux[1:2], axis=1, keepdims=True))
                   - jnp.exp(jnp.sum(aux[2:3] * aux[3:4], axis=1, keepdims=True)) + diff_lam_init)
            gain = aux[4:5, :dv]
            for h in range(N_HEADS):
                o = normalised(2 * h) - lam * normalised(2 * h + 1)
                ms = jnp.mean(o * o, axis=1, keepdims=True)
                outs.append(o * lax.rsqrt(ms + 1e-5) * gain * (1.0 - diff_lam_init))
        o_ref[...] = jnp.concatenate(outs, axis=1).astype(o_ref.dtype)


def _flash(q, k, v, aux, batch, heads, diff_lam_init):
    t, wq = q.shape
    s = t // batch
    tq = TQ
    nq = s // tq
    pairs = [(a, b) for a in range(nq) for b in range(a + 1)]
    qi_tab = jnp.asarray([p[0] for p in pairs], I32)
    ki_tab = jnp.asarray([p[1] for p in pairs], I32)
    nv = len(heads)
    dv = GROUP_WIDTH // N_HEADS
    grid_spec = pltpu.PrefetchScalarGridSpec(
        num_scalar_prefetch=2,
        grid=(batch, len(pairs)),
        in_specs=[pl.BlockSpec((tq, wq), lambda b, i, qt, kt: (b * nq + qt[i], 0)),
                  pl.BlockSpec((tq, wq), lambda b, i, qt, kt: (b * nq + kt[i], 0)),
                  pl.BlockSpec((tq, V_EXT), lambda b, i, qt, kt: (b * nq + kt[i], 0)),
                  pl.BlockSpec(aux.shape, lambda b, i, qt, kt: (0, 0))],
        out_specs=pl.BlockSpec((tq, 256), lambda b, i, qt, kt: (b * nq + qt[i], 0)),
        scratch_shapes=[pltpu.VMEM((nv, tq, LANES), BF16), pltpu.VMEM((nv, tq, LANES), F32),
                        pltpu.VMEM((nv, tq, LANES), F32)])
    return pl.pallas_call(
        functools.partial(_flash_kernel, heads=heads, diff_lam_init=diff_lam_init),
        grid_spec=grid_spec,
        out_shape=jax.ShapeDtypeStruct((t, 256), BF16),
        compiler_params=_cparams(("parallel", "arbitrary")),
    )(qi_tab, ki_tab, q, k, v, aux)


DIFF_HEADS_SPEC = tuple(((2 * h + m) // 4, 32 * ((2 * h + m) % 4), 32, h)
                        for h in range(N_HEADS) for m in range(2))
MLA_HEADS_SPEC = tuple((h, 0, LANES, h) for h in range(N_HEADS))


def _layer_norm(z, g, b):
    mu = jnp.mean(z, axis=1, keepdims=True)
    zc = z - mu
    var = jnp.mean(zc * zc, axis=1, keepdims=True)
    return zc * lax.rsqrt(var + NORM_EPS) * g + b


def _outproj_kernel(a_ref, p_ref, c_ref, d_ref, x_ref, w_ref, g_ref, b_ref, rwh_ref, rwl_ref, rb_ref, tri_ref,
                    h_ref, idx_ref, gate_ref, rank_ref, cnt_ref):
    y = jnp.zeros(h_ref.shape, F32)
    for i, part in enumerate((a_ref, p_ref, c_ref, d_ref)):
        y = y + jnp.dot(part[...], w_ref[i * GROUP_WIDTH:(i + 1) * GROUP_WIDTH, :],
                        preferred_element_type=F32)
    h = _layer_norm(DEEPNORM_ALPHA * x_ref[...] + y, g_ref[...], b_ref[...])
    h_ref[...] = h

    hi = h.astype(BF16)
    lo = (h - hi.astype(F32)).astype(BF16)
    logits = (jnp.dot(hi, rwh_ref[...], preferred_element_type=F32)
              + jnp.dot(lo, rwh_ref[...], preferred_element_type=F32)
              + jnp.dot(hi, rwl_ref[...], preferred_element_type=F32) + rb_ref[...])
    lane = _lane_iota(logits.shape)
    logits = jnp.where(lane < N_EXPERTS, logits, NEG)
    idx_out = jnp.zeros(logits.shape, I32)
    chosen = jnp.zeros(logits.shape, F32)
    top, picks = [], []
    for r in range(TOP_K):
        m = jnp.max(logits, axis=1, keepdims=True)
        idx = jnp.min(jnp.where(logits == m, lane, LANES), axis=1, keepdims=True)
        idx_out = jnp.where(lane == r, idx, idx_out)
        pick = lane == idx
        chosen = jnp.where(pick, 1.0, chosen)
        logits = jnp.where(pick, NEG, logits)
        top.append(m)
        picks.append(pick)
    es = [jnp.exp(m - top[0]) for m in top]
    denom = es[0] + es[1] + es[2] + es[3]
    gates = jnp.zeros(logits.shape, F32)
    for r in range(TOP_K):
        gates = jnp.where(lane == r, es[r] / denom, gates)
    idx_ref[...] = idx_out
    gate_ref[...] = gates

    seen = jnp.dot(tri_ref[...], chosen.astype(BF16), preferred_element_type=F32)
    rank = jnp.zeros(logits.shape, F32)
    for r in range(TOP_K):
        before = jnp.sum(jnp.where(picks[r], seen, 0.0), axis=1, keepdims=True) - 1.0
        rank = jnp.where(lane == r, before, rank)
    rank_ref[...] = rank.astype(I32)
    cnt_ref[0] = seen[seen.shape[0] - 8:, :].astype(I32)


def _outproj(parts, x2, w_out, g, b, rwh, rwl, rb, tri):
    t = x2.shape[0]
    tm = TM_OUT
    row = lambda w: pl.BlockSpec((tm, w), lambda i: (i, 0))
    full = lambda a: pl.BlockSpec(a.shape, lambda i: (0,) * a.ndim)
    return pl.pallas_call(
        _outproj_kernel,
        grid=(t // tm,),
        in_specs=[row(256)] * 4 + [row(D_MODEL), full(w_out), full(g), full(b), full(rwh), full(rwl), full(rb),
                                   full(tri)],
        out_specs=[row(D_MODEL), row(LANES), row(LANES), row(LANES),
                   pl.BlockSpec((1, 8, LANES), lambda i: (i, 0, 0))],
        out_shape=[jax.ShapeDtypeStruct((t, D_MODEL), F32), jax.ShapeDtypeStruct((t, LANES), I32),
                   jax.ShapeDtypeStruct((t, LANES), F32), jax.ShapeDtypeStruct((t, LANES), I32),
                   jax.ShapeDtypeStruct((t // tm, 8, LANES), I32)],
        compiler_params=_cparams(("parallel",)),
    )(*parts, x2, w_out, g, b, rwh, rwl, rb, tri)


def _route(top_idx, rank, tile_cnt, bm, tm):
    t = top_idx.shape[0]
    n_tiles = t // tm
    tile_cnt = tile_cnt[:, :N_EXPERTS]
    tile_base = jnp.cumsum(tile_cnt, axis=0) - tile_cnt
    counts = jnp.sum(tile_cnt, axis=0)
    nblk_e = (counts + bm - 1) // bm
    blk_end = jnp.cumsum(nblk_e)
    blk_start = blk_end - nblk_e
    offs = blk_start[None, :] * bm + tile_base
    onehot = top_idx.reshape(n_tiles, tm, TOP_K, 1) == jnp.arange(N_EXPERTS, dtype=I32)
    dest = rank.reshape(n_tiles, tm, TOP_K) + jnp.sum(jnp.where(onehot, offs[:, None, None, :], 0), axis=-1)
    n_blocks = (t * TOP_K) // bm + N_EXPERTS
    blk = jnp.arange(n_blocks, dtype=I32)
    block_e = jnp.minimum(jnp.sum(blk[:, None] >= blk_end[None, :], axis=1), N_EXPERTS - 1).astype(I32)
    nb = blk_end[-1:].astype(I32)
    cnt = jnp.clip(counts[block_e] - (blk - blk_start[block_e]) * bm, 0, bm)
    cnt = jnp.where(blk < nb[0], cnt, 0).astype(I32)
    return dest.reshape(t * TOP_K).astype(I32), block_e, nb, cnt


def _dispatch_kernel(cnt_ref, dest_ref, h_ref, xs_hbm, stage, zbuf, sem, *, tm, n_steps, bm, n_blocks):
    i = pl.program_id(0)

    @pl.when(i == 0)
    def _():
        zbuf[...] = jnp.zeros_like(zbuf)

        def zero_blocks(act):
            def body(blk, c):
                @pl.when(cnt_ref[blk] < bm)
                def _():
                    act(pltpu.make_async_copy(zbuf, xs_hbm.at[pl.ds(pl.multiple_of(blk * bm, bm), bm)], sem.at[2]))
                return c
            lax.fori_loop(0, n_blocks, body, 0)

        zero_blocks(lambda cp: cp.start())
        zero_blocks(lambda cp: cp.wait())

    def copy(r, dst, s):
        return pltpu.make_async_copy(stage.at[s, pl.ds(r, 1)], xs_hbm.at[pl.ds(dst, 1)], sem.at[s])

    def drain(s):
        for _ in range(tm * TOP_K):
            copy(0, 0, s).wait()

    for s in range(2):
        @pl.when(i >= 1)
        def _():
            drain(s)

        stage[s] = h_ref[s * tm:(s + 1) * tm, :]
        for r in range(tm):
            for k in range(TOP_K):
                dst = dest_ref[0, 0, (s * tm + r) * TOP_K + k]
                copy(r, dst, s).start(priority=k % 2)

    @pl.when(i == n_steps - 1)
    def _():
        drain(0)
        drain(1)


def _dispatch(h, dest, cnt, n_rows):
    t = h.shape[0]
    tm = TM_DISP
    bm = BM_MOE
    n_steps = t // (2 * tm)
    grid_spec = pltpu.PrefetchScalarGridSpec(
        num_scalar_prefetch=1,
        grid=(n_steps,),
        in_specs=[pl.BlockSpec((1, 1, 2 * tm * TOP_K), lambda i, *_: (i, 0, 0), memory_space=pltpu.SMEM),
                  pl.BlockSpec((2 * tm, D_MODEL), lambda i, *_: (i, 0))],
        out_specs=pl.BlockSpec(memory_space=pl.ANY),
        scratch_shapes=[pltpu.VMEM((2, tm, D_MODEL), F32), pltpu.VMEM((bm, D_MODEL), F32),
                        pltpu.SemaphoreType.DMA((3,))])
    return pl.pallas_call(
        functools.partial(_dispatch_kernel, tm=tm, n_steps=n_steps, bm=bm, n_blocks=n_rows // bm),
        grid_spec=grid_spec,
        out_shape=jax.ShapeDtypeStruct((n_rows, D_MODEL), F32),
        compiler_params=_cparams(("arbitrary",)),
    )(cnt, dest.reshape(n_steps, 1, 2 * tm * TOP_K), h)


def _expert_kernel(be_ref, nb_ref, x_ref, wgu_ref, bgu_ref, wd_ref, bd_ref, y_ref, wgu_bf, wd_bf):
    i = pl.program_id(0)
    live = i < nb_ref[0]
    first_of_expert = (i == 0) | (be_ref[i] != be_ref[jnp.maximum(i - 1, 0)])

    @pl.when(live & first_of_expert)
    def _():
        wgu_bf[...] = wgu_ref[0].astype(BF16)
        wd_bf[...] = wd_ref[0].astype(BF16)

    @pl.when(live)
    def _():
        x = x_ref[...].astype(BF16)
        acc = jnp.zeros((x.shape[0], D_MODEL), F32)
        for j in range(D_EXPERT // HC_MOE):
            lo = j * HC_MOE
            gate = (jnp.dot(x, wgu_bf[:, lo:lo + HC_MOE], preferred_element_type=F32)
                    + bgu_ref[0, :, lo:lo + HC_MOE])
            up = (jnp.dot(x, wgu_bf[:, D_EXPERT + lo:D_EXPERT + lo + HC_MOE], preferred_element_type=F32)
                  + bgu_ref[0, :, D_EXPERT + lo:D_EXPERT + lo + HC_MOE])
            gate = jnp.minimum(gate, SWIGLU_LIMIT)
            up = jnp.clip(up, -SWIGLU_LIMIT, SWIGLU_LIMIT)
            act = (up + 1.0) * gate * (1.0 / (1.0 + jnp.exp(-SWIGLU_ALPHA * gate)))
            acc = acc + jnp.dot(act.astype(BF16), wd_bf[lo:lo + HC_MOE, :], preferred_element_type=F32)
        y_ref[...] = acc + bd_ref[0]

    @pl.when(jnp.logical_not(live))
    def _():
        y_ref[...] = jnp.zeros_like(y_ref)


def _experts(xs, block_e, nb, wgu, bgu, wd, bd, layer):
    bm = BM_MOE
    n_blocks = xs.shape[0] // bm
    per_expert = lambda *dims: pl.BlockSpec((None, 1) + dims, lambda i, be, *_: (layer, be[i], 0, 0))
    grid_spec = pltpu.PrefetchScalarGridSpec(
        num_scalar_prefetch=2,
        grid=(n_blocks,),
        in_specs=[pl.BlockSpec((bm, D_MODEL), lambda i, *_: (i, 0)),
                  per_expert(D_MODEL, 2 * D_EXPERT), per_expert(1, 2 * D_EXPERT),
                  per_expert(D_EXPERT, D_MODEL), per_expert(1, D_MODEL)],
        out_specs=pl.BlockSpec((bm, D_MODEL), lambda i, *_: (i, 0)),
        scratch_shapes=[pltpu.VMEM((D_MODEL, 2 * D_EXPERT), BF16), pltpu.VMEM((D_EXPERT, D_MODEL), BF16)])
    return pl.pallas_call(
        _expert_kernel,
        grid_spec=grid_spec,
        out_shape=jax.ShapeDtypeStruct(xs.shape, F32),
        compiler_params=_cparams(("arbitrary",)),
    )(block_e, nb, xs, wgu, bgu, wd, bd)


def _combine_kernel(dest_ref, nxt_ref, gate_ref, h_ref, g_ref, b_ref, ys_hbm, o_ref, buf, sem, *, tm, n_steps):
    i = pl.program_id(0)

    def copy(src, r, k, s):
        return pltpu.make_async_copy(ys_hbm.at[pl.ds(src, 1)], buf.at[s, k, pl.ds(r, 1)], sem.at[s])

    def issue(idx_ref, half, s):
        for r in range(tm):
            for k in range(TOP_K):
                src = idx_ref[0, 0, (half * tm + r) * TOP_K + k]
                copy(src, r, k, s).start(priority=k % 2)

    def finish(half, s):
        for _ in range(tm * TOP_K):
            copy(0, 0, 0, s).wait()
        rows = slice(half * tm, (half + 1) * tm)
        gates = gate_ref[rows, :]
        f = jnp.zeros((tm, D_MODEL), F32)
        for k in range(TOP_K):
            f = f + gates[:, k:k + 1] * buf[s, k]
        o_ref[rows, :] = _layer_norm(DEEPNORM_ALPHA * h_ref[rows, :] + f, g_ref[...], b_ref[...])

    @pl.when(i == 0)
    def _():
        issue(dest_ref, 0, 0)

    issue(dest_ref, 1, 1)
    finish(0, 0)

    @pl.when(i + 1 < n_steps)
    def _():
        issue(nxt_ref, 0, 0)

    finish(1, 1)


def _combine(ys, dest, gates, h, g, b):
    t = h.shape[0]
    tm = TM_COMB
    n_steps = t // (2 * tm)
    row = lambda w: pl.BlockSpec((2 * tm, w), lambda i: (i, 0))
    full = lambda a: pl.BlockSpec(a.shape, lambda i: (0,) * a.ndim)
    idx_spec = lambda f: pl.BlockSpec((1, 1, 2 * tm * TOP_K), f, memory_space=pltpu.SMEM)
    dest3 = dest.reshape(n_steps, 1, 2 * tm * TOP_K)
    return pl.pallas_call(
        functools.partial(_combine_kernel, tm=tm, n_steps=n_steps),
        grid=(n_steps,),
        in_specs=[idx_spec(lambda i: (i, 0, 0)), idx_spec(lambda i: (jnp.minimum(i + 1, n_steps - 1), 0, 0)),
                  row(LANES), row(D_MODEL), full(g), full(b), pl.BlockSpec(memory_space=pl.ANY)],
        out_specs=row(D_MODEL),
        out_shape=jax.ShapeDtypeStruct((t, D_MODEL), F32),
        scratch_shapes=[pltpu.VMEM((2, TOP_K, tm, D_MODEL), F32), pltpu.SemaphoreType.DMA((2,))],
        compiler_params=_cparams(("arbitrary",)),
    )(dest3, dest3, gates, h, g, b, ys)


def _pad_cols(w, width):
    return jnp.pad(w, ((0, 0), (0, width - w.shape[1])))


def _pack_layer(w_in, gla_w_gate, gla_b_gate, gla_norm, pool_w, pool_scale, diff_lq1, diff_lk1,
                diff_lq2, diff_lk2, diff_norm, mla_q_norm, mla_w_uq, mla_kv_norm, mla_w_ukv,
                router_w, router_b):
    offs = [0, 128, 256, 512, 768, 784, 1040, 1296, 1552, 1808, 2000, 2160]
    seg = lambda i: w_in[:, offs[i]:offs[i + 1]]
    dvh = GROUP_WIDTH // N_HEADS
    spread_heads = lambda w: jnp.concatenate(
        [_pad_cols(w[:, h * dvh:(h + 1) * dvh], LANES) for h in range(N_HEADS)], axis=1)
    w_big = jnp.concatenate(
        [seg(0), seg(1), seg(2), seg(3), _pad_cols(seg(4), LANES), seg(5), seg(6), seg(7), spread_heads(seg(8)),
         _pad_cols(seg(9), 256), _pad_cols(seg(10), 256)], axis=1).astype(BF16)
    wg = jnp.pad(gla_w_gate, ((0, LANES - GLA_GATE_RANK), (0, 0))).astype(BF16)
    bg = gla_b_gate.reshape(1, LANES)
    gnorm = jnp.tile(gla_norm, N_HEADS).reshape(1, 256)
    eye = jnp.eye(N_HEADS, dtype=F32)
    ind = jnp.kron(eye, jnp.ones((GLA_DK, GLA_DV), F32)).astype(BF16)
    tri = jnp.tril(jnp.ones((GLA_CHUNK, GLA_CHUNK), F32)).astype(BF16)
    pool_bd = jax.scipy.linalg.block_diag(*[pool_w[g] for g in range(4)]).astype(BF16)
    pscale = pool_scale.reshape(1, 256)
    aux = jnp.zeros((8, LANES), F32)
    aux = aux.at[0, :DIFF_DH].set(diff_lq1).at[1, :DIFF_DH].set(diff_lk1)
    aux = aux.at[2, :DIFF_DH].set(diff_lq2).at[3, :DIFF_DH].set(diff_lk2)
    aux = aux.at[4, :2 * DIFF_DH].set(diff_norm)
    qn = _pad_cols(mla_q_norm.reshape(1, MLA_Q_RANK), 256)
    kvn = mla_kv_norm.reshape(1, MLA_KV_RANK)
    dqk = MLA_NOPE + MLA_ROPE
    wuq = jnp.concatenate([_pad_cols(mla_w_uq[:, h * dqk:(h + 1) * dqk], LANES) for h in range(N_HEADS)], axis=1)
    wuq = jnp.pad(wuq, ((0, 256 - MLA_Q_RANK), (0, 0))).astype(BF16)
    dkv = MLA_NOPE + MLA_V
    wk = jnp.concatenate([_pad_cols(mla_w_ukv[:, h * dkv:h * dkv + MLA_NOPE], LANES) for h in range(N_HEADS)], axis=1)
    wv = jnp.concatenate([mla_w_ukv[:, h * dkv + MLA_NOPE:(h + 1) * dkv] for h in range(N_HEADS)], axis=1)
    wukv = jnp.concatenate([wk, spread_heads(wv)], axis=1).astype(BF16)
    sel = jnp.zeros((LANES, 4 * LANES), F32)
    for h in range(N_HEADS):
        sel = sel.at[jnp.arange(MLA_ROPE), h * LANES + MLA_NOPE + jnp.arange(MLA_ROPE)].set(1.0)
    sel = sel.astype(BF16)
    rw = _pad_cols(router_w, LANES)
    rwh = rw.astype(BF16)
    rwl = (rw - rwh.astype(F32)).astype(BF16)
    rb = _pad_cols(router_b.reshape(1, N_EXPERTS), LANES)
    return dict(w_big=w_big, wg=wg, bg=bg, gnorm=gnorm, ind=ind, tri=tri, pool_bd=pool_bd, pscale=pscale,
                aux=aux, qn=qn, kvn=kvn, wuq=wuq, wukv=wukv, sel=sel, rwh=rwh, rwl=rwl, rb=rb)


def kernel(x, positions, w_in, gla_w_gate, gla_b_gate, gla_norm, pool_w, pool_scale, diff_lq1, diff_lk1,
           diff_lq2, diff_lk2, diff_norm, mla_q_norm, mla_w_uq, mla_kv_norm, mla_w_ukv, w_out, ln1_g, ln1_b,
           router_w, router_b, w_gate_up, b_gate_up, w_down, b_down, ln2_g, ln2_b):
    batch, seq, d = x.shape
    t = batch * seq
    assert d == D_MODEL and seq % TQ == 0 and t % TM_PROJ == 0 and (t * TOP_K) % BM_MOE == 0

    inv = 1.0 / (ROPE_THETA ** (jnp.arange(0, DIFF_DH, 2, dtype=F32) / DIFF_DH))
    ang = positions.astype(F32).reshape(t, 1) * inv[None, :]
    cos, sin = jnp.cos(ang), jnp.sin(ang)
    cos128 = jnp.tile(jnp.concatenate([cos, cos], axis=1), (1, 4))
    sin128 = jnp.tile(jnp.concatenate([-sin, sin], axis=1), (1, 4))

    tri_out = jnp.tril(jnp.ones((TM_OUT, TM_OUT), F32)).astype(BF16)

    h = x.reshape(t, d)
    for l in range(DEPTH):
        p = _pack_layer(w_in[l], gla_w_gate[l], gla_b_gate[l], gla_norm[l], pool_w[l], pool_scale[l],
                        diff_lq1[l], diff_lk1[l], diff_lq2[l], diff_lk2[l], diff_norm[l], mla_q_norm[l],
                        mla_w_uq[l], mla_kv_norm[l], mla_w_ukv[l], router_w[l], router_b[l])
        gla_in, pool_in, dq, dk, dv, mq, mk, mv = _proj(
            h, p['w_big'], cos128, sin128, p['qn'], p['kvn'], p['wuq'], p['wukv'], p['sel'])
        gla_out = _gla(gla_in, p['wg'], p['bg'], p['gnorm'], p['ind'], p['tri'], batch)
        pool_out = _pool(pool_in, p['pool_bd'], p['pscale'], batch)
        lam_init = 0.8 - 0.6 * math.exp(-0.3 * l)
        diff_out = _flash(dq, dk, dv, p['aux'], batch, DIFF_HEADS_SPEC, lam_init)
        mla_out = _flash(mq, mk, mv, p['aux'], batch, MLA_HEADS_SPEC, None)
        h1, top_idx, gates, rank, tile_cnt = _outproj(
            (gla_out, pool_out, diff_out, mla_out), h, w_out[l].astype(BF16), ln1_g[l].reshape(1, d),
            ln1_b[l].reshape(1, d), p['rwh'], p['rwl'], p['rb'], tri_out)
        dest, block_e, nb, cnt = _route(top_idx[:, :TOP_K], rank[:, :TOP_K], tile_cnt[:, 7, :], BM_MOE, TM_OUT)
        n_rows = t * TOP_K + N_EXPERTS * BM_MOE
        xs = _dispatch(h1, dest, cnt, n_rows)
        ys = _experts(xs, block_e, nb, w_gate_up, b_gate_up.reshape(DEPTH, N_EXPERTS, 1, 2 * D_EXPERT),
                      w_down, b_down.reshape(DEPTH, N_EXPERTS, 1, D_MODEL), l)
        h = _combine(ys, dest, gates, h1, ln2_g[l].reshape(1, d), ln2_b[l].reshape(1, d))
    return h.reshape(batch, seq, d)
```

```python
import functools
import math

import jax
import jax.numpy as jnp
from jax import lax
from jax.experimental import pallas as pl
from jax.experimental.pallas import tpu as pltpu

F32 = jnp.float32
BF16 = jnp.bfloat16
I32 = jnp.int32

D_MODEL = 1024
DEPTH = 2
GROUP_WIDTH = 256
N_HEADS = 4
GLA_DK = 32
GLA_DV = 64
GLA_GATE_RANK = 16
GLA_GATE_NORM = 16.0
POOL_WINDOWS = (2, 4, 8, 16)
POOL_CH = 64
DIFF_DH = 32
MLA_Q_RANK = 192
MLA_KV_RANK = 128
MLA_NOPE = 64
MLA_ROPE = 32
MLA_V = 64
ROPE_THETA = 10000.0
N_EXPERTS = 32
TOP_K = 4
D_EXPERT = 1024
SWIGLU_LIMIT = 7.0
SWIGLU_ALPHA = 1.702
NORM_EPS = 1e-5
DEEPNORM_ALPHA = (2 * DEPTH) ** 0.25

LANES = 128
MXU_WIDTH = 256
VMEM_LIMIT = 56 * 1024 * 1024

C_GLA = 0
C_POOL = 896
C_DQ = 1152
C_DK = 1408
C_DV = 1664
C_MQA = 2176
C_MKVA = 2432
C_END = 2688
GLA_W = C_POOL - C_GLA
V_EXT = 4 * LANES

NEG = -1e30
LOG2E = math.log2(math.e)

TM_PROJ = 512
TG_GLA = 512
GLA_CHUNK = 64
GLA_SUB = 16
TQ = 512
TM_OUT = 512
BM_MOE = 512
HC_MOE = 512
TM_DISP = 128
TM_COMB = 128


def _cparams(sem):
    return pltpu.CompilerParams(dimension_semantics=sem, vmem_limit_bytes=VMEM_LIMIT)


def _lane_iota(shape):
    return lax.broadcasted_iota(I32, shape, len(shape) - 1)


def _row_iota(shape):
    return lax.broadcasted_iota(I32, shape, len(shape) - 2)


def _rope(x, cos, sin_signed):
    w = x.shape[-1]
    from_low = pltpu.roll(x, 16, axis=1)
    from_high = pltpu.roll(x, w - 16, axis=1)
    swapped = jnp.where((_lane_iota(x.shape) % 32) >= 16, from_low, from_high)
    return x * cos + swapped * sin_signed


def _proj_kernel(x_ref, w_ref, cos_ref, sin_ref, qn_ref, kvn_ref, wuq_ref, wukv_ref, sel_ref,
                 gla_ref, pool_ref, dq_ref, dk_ref, dv_ref, mq_ref, mk_ref, mv_ref):
    xb = x_ref[...].astype(BF16)

    def seg(lo, hi):
        return jnp.dot(xb, w_ref[:, lo:hi], preferred_element_type=F32)

    gla_ref[...] = seg(C_GLA, C_POOL)
    pool_ref[...] = seg(C_POOL, C_DQ)

    cos = cos_ref[...]
    sin = sin_ref[...]
    cos2 = jnp.concatenate([cos, cos], axis=1)
    sin2 = jnp.concatenate([sin, sin], axis=1)
    dq_ref[...] = (_rope(seg(C_DQ, C_DK), cos2, sin2) * (LOG2E * DIFF_DH ** -0.5)).astype(BF16)
    dk_ref[...] = _rope(seg(C_DK, C_DV), cos2, sin2).astype(BF16)
    ones_cols = ((_lane_iota((1, V_EXT)) % LANES) >= GROUP_WIDTH // N_HEADS).astype(F32)
    dv_ref[...] = (seg(C_DV, C_MQA) + ones_cols).astype(BF16)

    mqa = seg(C_MQA, C_MKVA)
    ms = jnp.sum(mqa * mqa, axis=1, keepdims=True) * (1.0 / MLA_Q_RANK)
    cq = (mqa * lax.rsqrt(ms + 1e-6) * qn_ref[...]).astype(BF16)
    mq = jnp.dot(cq, wuq_ref[...], preferred_element_type=F32)
    cos4 = jnp.concatenate([cos2, cos2], axis=1)
    sin4 = jnp.concatenate([sin2, sin2], axis=1)
    lane = _lane_iota(mq.shape) % LANES
    is_rope = (lane >= MLA_NOPE) & (lane < MLA_NOPE + MLA_ROPE)
    mq = jnp.where(is_rope, _rope(mq, cos4, sin4), mq)
    mq_ref[...] = (mq * (LOG2E * (MLA_NOPE + MLA_ROPE) ** -0.5)).astype(BF16)

    mkva = seg(C_MKVA, C_END)
    ckv_raw = mkva[:, :LANES]
    msk = jnp.sum(ckv_raw * ckv_raw, axis=1, keepdims=True) * (1.0 / MLA_KV_RANK)
    ckv = (ckv_raw * lax.rsqrt(msk + 1e-6) * kvn_ref[...]).astype(BF16)
    kv = jnp.dot(ckv, wukv_ref[...], preferred_element_type=F32)
    k_rope = _rope(mkva[:, LANES:], cos, sin).astype(BF16)
    placed = jnp.dot(k_rope, sel_ref[...], preferred_element_type=F32)
    mk_ref[...] = (kv[:, :4 * LANES] + placed).astype(BF16)
    mv_ref[...] = (kv[:, 4 * LANES:] + ones_cols).astype(BF16)


def _proj(x2, w_big, cos128, sin128, qn, kvn, wuq, wukv, sel):
    t = x2.shape[0]
    tm = TM_PROJ
    row = lambda w: pl.BlockSpec((tm, w), lambda i: (i, 0))
    full = lambda a: pl.BlockSpec(a.shape, lambda i: (0,) * a.ndim)
    out_shapes = [
        jax.ShapeDtypeStruct((t, GLA_W), F32), jax.ShapeDtypeStruct((t, 256), F32),
        jax.ShapeDtypeStruct((t, 256), BF16), jax.ShapeDtypeStruct((t, 256), BF16),
        jax.ShapeDtypeStruct((t, V_EXT), BF16), jax.ShapeDtypeStruct((t, 512), BF16),
        jax.ShapeDtypeStruct((t, 512), BF16), jax.ShapeDtypeStruct((t, V_EXT), BF16)]
    return pl.pallas_call(
        _proj_kernel,
        grid=(t // tm,),
        in_specs=[row(D_MODEL), full(w_big), row(LANES), row(LANES), full(qn), full(kvn),
                  full(wuq), full(wukv), full(sel)],
        out_specs=[row(s.shape[1]) for s in out_shapes],
        out_shape=out_shapes,
        compiler_params=_cparams(("parallel",)),
    )(x2, w_big, cos128, sin128, qn, kvn, wuq, wukv, sel)


def _split3(x):
    hi = x.astype(BF16)
    r1 = x - hi.astype(F32)
    mid = r1.astype(BF16)
    lo = (r1 - mid.astype(F32)).astype(BF16)
    return hi, mid, lo


def _gla_kernel(blk_ref, wg_ref, bg_ref, norm_ref, ind_ref, tri_ref, out_ref, st_ref, b_scr, *, n_chunks):
    c_len = GLA_CHUNK

    @pl.when(pl.program_id(1) == 0)
    def _():
        st_ref[...] = jnp.zeros_like(st_ref)

    same_head = (_row_iota((256, LANES)) // GLA_DV) == (_lane_iota((256, LANES)) // GLA_DK)
    ind = ind_ref[...]
    tri = tri_ref[...]
    rows = _row_iota((8, LANES))

    n_sub = c_len // GLA_SUB
    same_head_kv = (_row_iota((LANES, 256)) // GLA_DK) == (_lane_iota((LANES, 256)) // GLA_DV)

    logit = jnp.dot(blk_ref[:, 768:896].astype(BF16), wg_ref[...], preferred_element_type=F32) + bg_ref[...]
    log_a = (jnp.minimum(logit, 0.0) - jnp.log(1.0 + jnp.exp(-jnp.abs(logit)))) * (1.0 / GLA_GATE_NORM)
    b3 = jnp.dot(tri, jnp.concatenate(_split3(log_a), axis=1), preferred_element_type=F32)
    b_scr[...] = b3[:, :LANES] + b3[:, LANES:2 * LANES] + b3[:, 2 * LANES:]

    def chunk(c, carry):
        r0 = pl.multiple_of(c * c_len, c_len)
        blk = blk_ref[pl.ds(r0, c_len), :]
        q = blk[:, 0:128] * (GLA_DK ** -0.5)
        k = blk[:, 128:256]
        v = blk[:, 256:512]
        g = blk[:, 512:768]
        b = b_scr[pl.ds(r0, c_len), :]

        st = st_ref[...]
        vb = v.astype(BF16)
        o0 = lax.dot_general((q * jnp.exp(b)).astype(BF16), st.astype(BF16), (((1,), (1,)), ((), ())),
                             preferred_element_type=F32)
        ends = [b[(u + 1) * GLA_SUB - 1:(u + 1) * GLA_SUB, :] for u in range(n_sub)]
        zero_sub = jnp.zeros((GLA_SUB, LANES), BF16)
        kd_rows = []
        for u in range(n_sub - 1):
            rs = slice(u * GLA_SUB, (u + 1) * GLA_SUB)
            kd_u = (k[rs, :] * jnp.exp(ends[u] - b[rs, :])).astype(BF16)
            kd_rows.append(jnp.concatenate([kd_u if w == u else zero_sub for w in range(n_sub - 1)], axis=1))
        kd_all = jnp.concatenate(kd_rows, axis=0)
        n_prev = (n_sub - 1) * GLA_SUB
        u_all = lax.dot_general(kd_all, vb[:n_prev, :], (((0,), (0,)), ((), ())), preferred_element_type=F32)
        u_all = jnp.concatenate(
            [jnp.where(same_head_kv, u_all[u * LANES:(u + 1) * LANES, :], 0.0) for u in range(n_sub - 1)],
            axis=0).astype(BF16)

        o_subs = []
        for sc in range(n_sub):
            lo = sc * GLA_SUB
            b_s, q_s, k_s, v_s = (a[lo:lo + GLA_SUB, :] for a in (b, q, k, v))
            e_tiles, owners = [], []
            for j in range(GLA_SUB):
                for t in range(j // 8, GLA_SUB // 8):
                    rs = slice(8 * t, 8 * (t + 1))
                    e = jnp.exp(b_s[rs, :] - b_s[j:j + 1, :]) * (q_s[rs, :] * k_s[j:j + 1, :])
                    if t == j // 8 and j % 8:
                        e = jnp.where(rows >= j % 8, e, 0.0)
                    e_tiles.append(e)
                    owners.append((j, t))
            e_all = jnp.concatenate(e_tiles, axis=0).astype(BF16)
            if sc == 0:
                p_all = jnp.dot(e_all, ind, preferred_element_type=F32)
                prev = None
            else:
                q_prev = jnp.concatenate([(q_s * jnp.exp(b_s - ends[u])).astype(BF16) for u in range(sc)], axis=1)
                top = jnp.concatenate([q_prev, zero_sub], axis=1)
                bottom = jnp.concatenate([jnp.zeros((e_all.shape[0], sc * LANES), BF16), e_all], axis=1)
                rhs = jnp.concatenate([u_all[:sc * LANES, :], ind], axis=0)
                both = jnp.dot(jnp.concatenate([top, bottom], axis=0), rhs, preferred_element_type=F32)
                prev = both[:GLA_SUB, :]
                p_all = both[GLA_SUB:, :]
            tiles = []
            for t in range(GLA_SUB // 8):
                rs = slice(8 * t, 8 * (t + 1))
                tiles.append(o0[lo + 8 * t:lo + 8 * (t + 1), :] + (prev[rs, :] if sc else 0.0))
            for n, (j, t) in enumerate(owners):
                tiles[t] = tiles[t] + p_all[8 * n:8 * (n + 1), :] * v_s[j:j + 1, :]
            o_subs.extend(tiles)
        o = jnp.concatenate(o_subs, axis=0)

        b_last = ends[n_sub - 1]
        kd_last = (k * jnp.exp(b_last - b)).astype(BF16)
        upd = lax.dot_general(vb, kd_last, (((0,), (0,)), ((), ())), preferred_element_type=F32)
        st_ref[...] = st * jnp.exp(b_last) + jnp.where(same_head, upd, 0.0)

        o2 = o * o
        head = _lane_iota(o.shape) // GLA_DV
        scale = jnp.zeros_like(o)
        for h in range(N_HEADS):
            ms = jnp.sum(jnp.where(head == h, o2, 0.0), axis=1, keepdims=True) * (1.0 / GLA_DV)
            scale = jnp.where(head == h, lax.rsqrt(ms + 1e-6), scale)
        silu = g / (1.0 + jnp.exp(-g))
        out_ref[pl.ds(r0, c_len), :] = (o * scale * norm_ref[...] * silu).astype(out_ref.dtype)
        return carry

    lax.fori_loop(0, n_chunks, chunk, 0, unroll=2)


def _gla(gla_in, wg, bg, norm, ind, tri, batch):
    t = gla_in.shape[0]
    s = t // batch
    tg = TG_GLA
    nblk = s // tg
    full = lambda a: pl.BlockSpec(a.shape, lambda b, i: (0,) * a.ndim)
    return pl.pallas_call(
        functools.partial(_gla_kernel, n_chunks=tg // GLA_CHUNK),
        grid=(batch, nblk),
        in_specs=[pl.BlockSpec((tg, GLA_W), lambda b, i: (b * nblk + i, 0)),
                  full(wg), full(bg), full(norm), full(ind), full(tri)],
        out_specs=pl.BlockSpec((tg, 256), lambda b, i: (b * nblk + i, 0)),
        out_shape=jax.ShapeDtypeStruct((t, 256), BF16),
        scratch_shapes=[pltpu.VMEM((256, LANES), F32), pltpu.VMEM((tg, LANES), F32)],
        compiler_params=_cparams(("parallel", "arbitrary")),
    )(gla_in, wg, bg, norm, ind, tri)


def _pool_kernel(u_ref, w_ref, scale_ref, out_ref):
    u = u_ref[...]
    t = _row_iota(u.shape)

    def shifted(x, k):
        return jnp.where(t >= k, pltpu.roll(x, k, axis=0), 0.0)

    sums = []
    acc = u
    for k in (1, 2, 4, 8):
        acc = acc + shifted(acc, k)
        sums.append(acc)
    group = _lane_iota(u.shape) // POOL_CH
    tf = (t + 1).astype(F32)
    mean = jnp.zeros_like(u)
    for gi, w in enumerate(POOL_WINDOWS):
        mean = jnp.where(group == gi, sums[gi] / jnp.minimum(tf, float(w)), mean)
    pooled = (mean - u).astype(BF16)
    y = jnp.dot(pooled, w_ref[...], preferred_element_type=F32) * scale_ref[...]
    out_ref[...] = y.astype(out_ref.dtype)


def _pool(pool_in, w_bd, scale, batch):
    t = pool_in.shape[0]
    s = t // batch
    full = lambda a: pl.BlockSpec(a.shape, lambda b: (0,) * a.ndim)
    return pl.pallas_call(
        _pool_kernel,
        grid=(batch,),
        in_specs=[pl.BlockSpec((s, 256), lambda b: (b, 0)), full(w_bd), full(scale)],
        out_specs=pl.BlockSpec((s, 256), lambda b: (b, 0)),
        out_shape=jax.ShapeDtypeStruct((t, 256), BF16),
        compiler_params=_cparams(("parallel",)),
    )(pool_in, w_bd, scale)


def _flash_kernel(qi_tab, ki_tab, q_ref, k_ref, v_ref, aux_ref, o_ref, qexp, m_s, acc_s,
                  *, heads, diff_lam_init):
    step = pl.program_id(1)
    qi = qi_tab[step]
    ki = ki_tab[step]
    tq = q_ref.shape[0]
    tk = k_ref.shape[0]
    dv = GROUP_WIDTH // N_HEADS

    @pl.when(ki == 0)
    def _():
        for vh, (grp, lo, width, _) in enumerate(heads):
            qg = q_ref[:, grp * LANES:(grp + 1) * LANES]
            lane = _lane_iota(qg.shape)
            qexp[vh] = jnp.where((lane >= lo) & (lane < lo + width), qg, jnp.zeros_like(qg))
        m_s[...] = jnp.full_like(m_s, NEG)
        acc_s[...] = jnp.zeros_like(acc_s)

    def update(masked):
        if masked:
            keep = _lane_iota((tq, tk)) <= _row_iota((tq, tk))
        for vh, (grp, _, _, hv) in enumerate(heads):
            kg = k_ref[:, grp * LANES:(grp + 1) * LANES]
            s = lax.dot_general(qexp[vh], kg, (((1,), (1,)), ((), ())), preferred_element_type=F32)
            if masked:
                s = jnp.where(keep, s, NEG)
            m_old = m_s[vh]
            m_new = jnp.maximum(m_old, jnp.max(s, axis=1, keepdims=True))
            alpha = jnp.exp2(m_old - m_new)
            p = jnp.exp2((s - jnp.concatenate([m_new] * (tk // LANES), axis=1)).astype(BF16))
            pv = jnp.dot(p, v_ref[:, hv * LANES:(hv + 1) * LANES], preferred_element_type=F32)
            acc_s[vh] = alpha * acc_s[vh] + pv
            m_s[vh] = m_new

    @pl.when(ki < qi)
    def _():
        update(False)

    @pl.when(ki == qi)
    def _():
        update(True)
        def normalised(vh):
            a = acc_s[vh]
            return (a / pltpu.roll(a, dv, axis=1))[:, :dv]

        outs = []
        if diff_lam_init is None:
            for vh in range(len(heads)):
                outs.append(normalised(vh))
        else:
            aux = aux_ref[...]
            lam = (jnp.exp(jnp.sum(aux[0:1] * aux[1:2], axis=1, keepdims=True))
                   - jnp.exp(jnp.sum(aux[2:3] * aux[3:4], axis=1, keepdims=True)) + diff_lam_init)
            gain = aux[4:5, :dv]
            for h in range(N_HEADS):
                o = normalised(2 * h) - lam * normalised(2 * h + 1)
                ms = jnp.mean(o * o, axis=1, keepdims=True)
                outs.append(o * lax.rsqrt(ms + 1e-5) * gain * (1.0 - diff_lam_init))
        o_ref[...] = jnp.concatenate(outs, axis=1).astype(o_ref.dtype)


def _flash(q, k, v, aux, batch, heads, diff_lam_init):
    t, wq = q.shape
    s = t // batch
    tq = TQ
    nq = s // tq
    pairs = [(a, b) for a in range(nq) for b in range(a + 1)]
    qi_tab = jnp.asarray([p[0] for p in pairs], I32)
    ki_tab = jnp.asarray([p[1] for p in pairs], I32)
    nv = len(heads)
    dv = GROUP_WIDTH // N_HEADS
    grid_spec = pltpu.PrefetchScalarGridSpec(
        num_scalar_prefetch=2,
        grid=(batch, len(pairs)),
        in_specs=[pl.BlockSpec((tq, wq), lambda b, i, qt, kt: (b * nq + qt[i], 0)),
                  pl.BlockSpec((tq, wq), lambda b, i, qt, kt: (b * nq + kt[i], 0)),
                  pl.BlockSpec((tq, V_EXT), lambda b, i, qt, kt: (b * nq + kt[i], 0)),
                  pl.BlockSpec(aux.shape, lambda b, i, qt, kt: (0, 0))],
        out_specs=pl.BlockSpec((tq, 256), lambda b, i, qt, kt: (b * nq + qt[i], 0)),
        scratch_shapes=[pltpu.VMEM((nv, tq, LANES), BF16), pltpu.VMEM((nv, tq, LANES), F32),
                        pltpu.VMEM((nv, tq, LANES), F32)])
    return pl.pallas_call(
        functools.partial(_flash_kernel, heads=heads, diff_lam_init=diff_lam_init),
        grid_spec=grid_spec,
        out_shape=jax.ShapeDtypeStruct((t, 256), BF16),
        compiler_params=_cparams(("parallel", "arbitrary")),
    )(qi_tab, ki_tab, q, k, v, aux)


DIFF_HEADS_SPEC = tuple(((2 * h + m) // 4, 32 * ((2 * h + m) % 4), 32, h)
                        for h in range(N_HEADS) for m in range(2))
MLA_HEADS_SPEC = tuple((h, 0, LANES, h) for h in range(N_HEADS))


def _layer_norm(z, g, b):
    mu = jnp.mean(z, axis=1, keepdims=True)
    zc = z - mu
    var = jnp.mean(zc * zc, axis=1, keepdims=True)
    return zc * lax.rsqrt(var + NORM_EPS) * g + b


def _outproj_kernel(a_ref, p_ref, c_ref, d_ref, x_ref, w_ref, g_ref, b_ref, rwh_ref, rwl_ref, rb_ref, tri_ref,
                    h_ref, idx_ref, gate_ref, rank_ref, cnt_ref):
    y = jnp.zeros(h_ref.shape, F32)
    for i, part in enumerate((a_ref, p_ref, c_ref, d_ref)):
        y = y + jnp.dot(part[...], w_ref[i * GROUP_WIDTH:(i + 1) * GROUP_WIDTH, :],
                        preferred_element_type=F32)
    h = _layer_norm(DEEPNORM_ALPHA * x_ref[...] + y, g_ref[...], b_ref[...])
    h_ref[...] = h

    hi = h.astype(BF16)
    lo = (h - hi.astype(F32)).astype(BF16)
    logits = (jnp.dot(hi, rwh_ref[...], preferred_element_type=F32)
              + jnp.dot(lo, rwh_ref[...], preferred_element_type=F32)
              + jnp.dot(hi, rwl_ref[...], preferred_element_type=F32) + rb_ref[...])
    lane = _lane_iota(logits.shape)
    logits = jnp.where(lane < N_EXPERTS, logits, NEG)
    idx_out = jnp.zeros(logits.shape, I32)
    chosen = jnp.zeros(logits.shape, F32)
    top, picks = [], []
    for r in range(TOP_K):
        m = jnp.max(logits, axis=1, keepdims=True)
        idx = jnp.min(jnp.where(logits == m, lane, LANES), axis=1, keepdims=True)
        idx_out = jnp.where(lane == r, idx, idx_out)
        pick = lane == idx
        chosen = jnp.where(pick, 1.0, chosen)
        logits = jnp.where(pick, NEG, logits)
        top.append(m)
        picks.append(pick)
    es = [jnp.exp(m - top[0]) for m in top]
    denom = es[0] + es[1] + es[2] + es[3]
    gates = jnp.zeros(logits.shape, F32)
    for r in range(TOP_K):
        gates = jnp.where(lane == r, es[r] / denom, gates)
    idx_ref[...] = idx_out
    gate_ref[...] = gates

    seen = jnp.dot(tri_ref[...], chosen.astype(BF16), preferred_element_type=F32)
    rank = jnp.zeros(logits.shape, F32)
    for r in range(TOP_K):
        before = jnp.sum(jnp.where(picks[r], seen, 0.0), axis=1, keepdims=True) - 1.0
        rank = jnp.where(lane == r, before, rank)
    rank_ref[...] = rank.astype(I32)
    cnt_ref[0] = seen[seen.shape[0] - 8:, :].astype(I32)


def _outproj(parts, x2, w_out, g, b, rwh, rwl, rb, tri):
    t = x2.shape[0]
    tm = TM_OUT
    row = lambda w: pl.BlockSpec((tm, w), lambda i: (i, 0))
    full = lambda a: pl.BlockSpec(a.shape, lambda i: (0,) * a.ndim)
    return pl.pallas_call(
        _outproj_kernel,
        grid=(t // tm,),
        in_specs=[row(256)] * 4 + [row(D_MODEL), full(w_out), full(g), full(b), full(rwh), full(rwl), full(rb),
                                   full(tri)],
        out_specs=[row(D_MODEL), row(LANES), row(LANES), row(LANES),
                   pl.BlockSpec((1, 8, LANES), lambda i: (i, 0, 0))],
        out_shape=[jax.ShapeDtypeStruct((t, D_MODEL), F32), jax.ShapeDtypeStruct((t, LANES), I32),
                   jax.ShapeDtypeStruct((t, LANES), F32), jax.ShapeDtypeStruct((t, LANES), I32),
                   jax.ShapeDtypeStruct((t // tm, 8, LANES), I32)],
        compiler_params=_cparams(("parallel",)),
    )(*parts, x2, w_out, g, b, rwh, rwl, rb, tri)


def _route(top_idx, rank, tile_cnt, bm, tm):
    t = top_idx.shape[0]
    n_tiles = t // tm
    tile_cnt = tile_cnt[:, :N_EXPERTS]
    tile_base = jnp.cumsum(tile_cnt, axis=0) - tile_cnt
    counts = jnp.sum(tile_cnt, axis=0)
    nblk_e = (counts + bm - 1) // bm
    blk_end = jnp.cumsum(nblk_e)
    blk_start = blk_end - nblk_e
    offs = blk_start[None, :] * bm + tile_base
    onehot = top_idx.reshape(n_tiles, tm, TOP_K, 1) == jnp.arange(N_EXPERTS, dtype=I32)
    dest = rank.reshape(n_tiles, tm, TOP_K) + jnp.sum(jnp.where(onehot, offs[:, None, None, :], 0), axis=-1)
    n_blocks = (t * TOP_K) // bm + N_EXPERTS
    blk = jnp.arange(n_blocks, dtype=I32)
    block_e = jnp.minimum(jnp.sum(blk[:, None] >= blk_end[None, :], axis=1), N_EXPERTS - 1).astype(I32)
    nb = blk_end[-1:].astype(I32)
    cnt = jnp.clip(counts[block_e] - (blk - blk_start[block_e]) * bm, 0, bm)
    cnt = jnp.where(blk < nb[0], cnt, 0).astype(I32)
    return dest.reshape(t * TOP_K).astype(I32), block_e, nb, cnt


def _dispatch_kernel(cnt_ref, dest_ref, h_ref, xs_hbm, stage, zbuf, sem, *, tm, n_steps, bm, n_blocks):
    i = pl.program_id(0)

    @pl.when(i == 0)
    def _():
        zbuf[...] = jnp.zeros_like(zbuf)

        def zero_blocks(act):
            def body(blk, c):
                @pl.when(cnt_ref[blk] < bm)
                def _():
                    act(pltpu.make_async_copy(zbuf, xs_hbm.at[pl.ds(pl.multiple_of(blk * bm, bm), bm)], sem.at[2]))
                return c
            lax.fori_loop(0, n_blocks, body, 0)

        zero_blocks(lambda cp: cp.start())
        zero_blocks(lambda cp: cp.wait())

    def copy(r, dst, s):
        return pltpu.make_async_copy(stage.at[s, pl.ds(r, 1)], xs_hbm.at[pl.ds(dst, 1)], sem.at[s])

    def drain(s):
        for _ in range(tm * TOP_K):
            copy(0, 0, s).wait()

    for s in range(2):
        @pl.when(i >= 1)
        def _():
            drain(s)

        stage[s] = h_ref[s * tm:(s + 1) * tm, :]
        for r in range(tm):
            for k in range(TOP_K):
                dst = dest_ref[0, 0, (s * tm + r) * TOP_K + k]
                copy(r, dst, s).start(priority=k % 2)

    @pl.when(i == n_steps - 1)
    def _():
        drain(0)
        drain(1)


def _dispatch(h, dest, cnt, n_rows):
    t = h.shape[0]
    tm = TM_DISP
    bm = BM_MOE
    n_steps = t // (2 * tm)
    grid_spec = pltpu.PrefetchScalarGridSpec(
        num_scalar_prefetch=1,
        grid=(n_steps,),
        in_specs=[pl.BlockSpec((1, 1, 2 * tm * TOP_K), lambda i, *_: (i, 0, 0), memory_space=pltpu.SMEM),
                  pl.BlockSpec((2 * tm, D_MODEL), lambda i, *_: (i, 0))],
        out_specs=pl.BlockSpec(memory_space=pl.ANY),
        scratch_shapes=[pltpu.VMEM((2, tm, D_MODEL), F32), pltpu.VMEM((bm, D_MODEL), F32),
                        pltpu.SemaphoreType.DMA((3,))])
    return pl.pallas_call(
        functools.partial(_dispatch_kernel, tm=tm, n_steps=n_steps, bm=bm, n_blocks=n_rows // bm),
        grid_spec=grid_spec,
        out_shape=jax.ShapeDtypeStruct((n_rows, D_MODEL), F32),
        compiler_params=_cparams(("arbitrary",)),
    )(cnt, dest.reshape(n_steps, 1, 2 * tm * TOP_K), h)


def _expert_kernel(be_ref, nb_ref, x_ref, wgu_ref, bgu_ref, wd_ref, bd_ref, y_ref, wgu_bf, wd_bf):
    i = pl.program_id(0)
    live = i < nb_ref[0]
    first_of_expert = (i == 0) | (be_ref[i] != be_ref[jnp.maximum(i - 1, 0)])

    @pl.when(live & first_of_expert)
    def _():
        wgu_bf[...] = wgu_ref[0].astype(BF16)
        wd_bf[...] = wd_ref[0].astype(BF16)

    @pl.when(live)
    def _():
        x = x_ref[...].astype(BF16)
        acc = jnp.zeros((x.shape[0], D_MODEL), F32)
        for j in range(D_EXPERT // HC_MOE):
            lo = j * HC_MOE
            gate = (jnp.dot(x, wgu_bf[:, lo:lo + HC_MOE], preferred_element_type=F32)
                    + bgu_ref[0, :, lo:lo + HC_MOE])
            up = (jnp.dot(x, wgu_bf[:, D_EXPERT + lo:D_EXPERT + lo + HC_MOE], preferred_element_type=F32)
                  + bgu_ref[0, :, D_EXPERT + lo:D_EXPERT + lo + HC_MOE])
            gate = jnp.minimum(gate, SWIGLU_LIMIT)
            up = jnp.clip(up, -SWIGLU_LIMIT, SWIGLU_LIMIT)
            act = (up + 1.0) * gate * (1.0 / (1.0 + jnp.exp(-SWIGLU_ALPHA * gate)))
            acc = acc + jnp.dot(act.astype(BF16), wd_bf[lo:lo + HC_MOE, :], preferred_element_type=F32)
        y_ref[...] = acc + bd_ref[0]

    @pl.when(jnp.logical_not(live))
    def _():
        y_ref[...] = jnp.zeros_like(y_ref)


def _experts(xs, block_e, nb, wgu, bgu, wd, bd, layer):
    bm = BM_MOE
    n_blocks = xs.shape[0] // bm
    per_expert = lambda *dims: pl.BlockSpec((None, 1) + dims, lambda i, be, *_: (layer, be[i], 0, 0))
    grid_spec = pltpu.PrefetchScalarGridSpec(
        num_scalar_prefetch=2,
        grid=(n_blocks,),
        in_specs=[pl.BlockSpec((bm, D_MODEL), lambda i, *_: (i, 0)),
                  per_expert(D_MODEL, 2 * D_EXPERT), per_expert(1, 2 * D_EXPERT),
                  per_expert(D_EXPERT, D_MODEL), per_expert(1, D_MODEL)],
        out_specs=pl.BlockSpec((bm, D_MODEL), lambda i, *_: (i, 0)),
        scratch_shapes=[pltpu.VMEM((D_MODEL, 2 * D_EXPERT), BF16), pltpu.VMEM((D_EXPERT, D_MODEL), BF16)])
    return pl.pallas_call(
        _expert_kernel,
        grid_spec=grid_spec,
        out_shape=jax.ShapeDtypeStruct(xs.shape, F32),
        compiler_params=_cparams(("arbitrary",)),
    )(block_e, nb, xs, wgu, bgu, wd, bd)


def _combine_kernel(dest_ref, nxt_ref, gate_ref, h_ref, g_ref, b_ref, ys_hbm, o_ref, buf, sem, *, tm, n_steps):
    i = pl.program_id(0)

    def copy(src, r, k, s):
        return pltpu.make_async_copy(ys_hbm.at[pl.ds(src, 1)], buf.at[s, k, pl.ds(r, 1)], sem.at[s])

    def issue(idx_ref, half, s):
        for r in range(tm):
            for k in range(TOP_K):
                src = idx_ref[0, 0, (half * tm + r) * TOP_K + k]
                copy(src, r, k, s).start(priority=k % 2)

    def finish(half, s):
        for _ in range(tm * TOP_K):
            copy(0, 0, 0, s).wait()
        rows = slice(half * tm, (half + 1) * tm)
        gates = gate_ref[rows, :]
        f = jnp.zeros((tm, D_MODEL), F32)
        for k in range(TOP_K):
            f = f + gates[:, k:k + 1] * buf[s, k]
        o_ref[rows, :] = _layer_norm(DEEPNORM_ALPHA * h_ref[rows, :] + f, g_ref[...], b_ref[...])

    @pl.when(i == 0)
    def _():
        issue(dest_ref, 0, 0)

    issue(dest_ref, 1, 1)
    finish(0, 0)

    @pl.when(i + 1 < n_steps)
    def _():
        issue(nxt_ref, 0, 0)

    finish(1, 1)


def _combine(ys, dest, gates, h, g, b):
    t = h.shape[0]
    tm = TM_COMB
    n_steps = t // (2 * tm)
    row = lambda w: pl.BlockSpec((2 * tm, w), lambda i: (i, 0))
    full = lambda a: pl.BlockSpec(a.shape, lambda i: (0,) * a.ndim)
    idx_spec = lambda f: pl.BlockSpec((1, 1, 2 * tm * TOP_K), f, memory_space=pltpu.SMEM)
    dest3 = dest.reshape(n_steps, 1, 2 * tm * TOP_K)
    return pl.pallas_call(
        functools.partial(_combine_kernel, tm=tm, n_steps=n_steps),
        grid=(n_steps,),
        in_specs=[idx_spec(lambda i: (i, 0, 0)), idx_spec(lambda i: (jnp.minimum(i + 1, n_steps - 1), 0, 0)),
                  row(LANES), row(D_MODEL), full(g), full(b), pl.BlockSpec(memory_space=pl.ANY)],
        out_specs=row(D_MODEL),
        out_shape=jax.ShapeDtypeStruct((t, D_MODEL), F32),
        scratch_shapes=[pltpu.VMEM((2, TOP_K, tm, D_MODEL), F32), pltpu.SemaphoreType.DMA((2,))],
        compiler_params=_cparams(("arbitrary",)),
    )(dest3, dest3, gates, h, g, b, ys)


def _pad_cols(w, width):
    return jnp.pad(w, ((0, 0), (0, width - w.shape[1])))


def _pack_layer(w_in, gla_w_gate, gla_b_gate, gla_norm, pool_w, pool_scale, diff_lq1, diff_lk1,
                diff_lq2, diff_lk2, diff_norm, mla_q_norm, mla_w_uq, mla_kv_norm, mla_w_ukv,
                router_w, router_b):
    offs = [0, 128, 256, 512, 768, 784, 1040, 1296, 1552, 1808, 2000, 2160]
    seg = lambda i: w_in[:, offs[i]:offs[i + 1]]
    dvh = GROUP_WIDTH // N_HEADS
    spread_heads = lambda w: jnp.concatenate(
        [_pad_cols(w[:, h * dvh:(h + 1) * dvh], LANES) for h in range(N_HEADS)], axis=1)
    w_big = jnp.concatenate(
        [seg(0), seg(1), seg(2), seg(3), _pad_cols(seg(4), LANES), seg(5), seg(6), seg(7), spread_heads(seg(8)),
         _pad_cols(seg(9), 256), _pad_cols(seg(10), 256)], axis=1).astype(BF16)
    wg = jnp.pad(gla_w_gate, ((0, LANES - GLA_GATE_RANK), (0, 0))).astype(BF16)
    bg = gla_b_gate.reshape(1, LANES)
    gnorm = jnp.tile(gla_norm, N_HEADS).reshape(1, 256)
    eye = jnp.eye(N_HEADS, dtype=F32)
    ind = jnp.kron(eye, jnp.ones((GLA_DK, GLA_DV), F32)).astype(BF16)
    tri = jnp.kron(jnp.eye(TG_GLA // GLA_CHUNK, dtype=F32),
                   jnp.tril(jnp.ones((GLA_CHUNK, GLA_CHUNK), F32))).astype(BF16)
    pool_bd = jax.scipy.linalg.block_diag(*[pool_w[g] for g in range(4)]).astype(BF16)
    pscale = pool_scale.reshape(1, 256)
    aux = jnp.zeros((8, LANES), F32)
    aux = aux.at[0, :DIFF_DH].set(diff_lq1).at[1, :DIFF_DH].set(diff_lk1)
    aux = aux.at[2, :DIFF_DH].set(diff_lq2).at[3, :DIFF_DH].set(diff_lk2)
    aux = aux.at[4, :2 * DIFF_DH].set(diff_norm)
    qn = _pad_cols(mla_q_norm.reshape(1, MLA_Q_RANK), 256)
    kvn = mla_kv_norm.reshape(1, MLA_KV_RANK)
    dqk = MLA_NOPE + MLA_ROPE
    wuq = jnp.concatenate([_pad_cols(mla_w_uq[:, h * dqk:(h + 1) * dqk], LANES) for h in range(N_HEADS)], axis=1)
    wuq = jnp.pad(wuq, ((0, 256 - MLA_Q_RANK), (0, 0))).astype(BF16)
    dkv = MLA_NOPE + MLA_V
    wk = jnp.concatenate([_pad_cols(mla_w_ukv[:, h * dkv:h * dkv + MLA_NOPE], LANES) for h in range(N_HEADS)], axis=1)
    wv = jnp.concatenate([mla_w_ukv[:, h * dkv + MLA_NOPE:(h + 1) * dkv] for h in range(N_HEADS)], axis=1)
    wukv = jnp.concatenate([wk, spread_heads(wv)], axis=1).astype(BF16)
    sel = jnp.zeros((LANES, 4 * LANES), F32)
    for h in range(N_HEADS):
        sel = sel.at[jnp.arange(MLA_ROPE), h * LANES + MLA_NOPE + jnp.arange(MLA_ROPE)].set(1.0)
    sel = sel.astype(BF16)
    rw = _pad_cols(router_w, LANES)
    rwh = rw.astype(BF16)
    rwl = (rw - rwh.astype(F32)).astype(BF16)
    rb = _pad_cols(router_b.reshape(1, N_EXPERTS), LANES)
    return dict(w_big=w_big, wg=wg, bg=bg, gnorm=gnorm, ind=ind, tri=tri, pool_bd=pool_bd, pscale=pscale,
                aux=aux, qn=qn, kvn=kvn, wuq=wuq, wukv=wukv, sel=sel, rwh=rwh, rwl=rwl, rb=rb)


def kernel(x, positions, w_in, gla_w_gate, gla_b_gate, gla_norm, pool_w, pool_scale, diff_lq1, diff_lk1,
           diff_lq2, diff_lk2, diff_norm, mla_q_norm, mla_w_uq, mla_kv_norm, mla_w_ukv, w_out, ln1_g, ln1_b,
           router_w, router_b, w_gate_up, b_gate_up, w_down, b_down, ln2_g, ln2_b):
    batch, seq, d = x.shape
    t = batch * seq
    assert d == D_MODEL and seq % TQ == 0 and t % TM_PROJ == 0 and (t * TOP_K) % BM_MOE == 0

    inv = 1.0 / (ROPE_THETA ** (jnp.arange(0, DIFF_DH, 2, dtype=F32) / DIFF_DH))
    ang = positions.astype(F32).reshape(t, 1) * inv[None, :]
    cos, sin = jnp.cos(ang), jnp.sin(ang)
    cos128 = jnp.tile(jnp.concatenate([cos, cos], axis=1), (1, 4))
    sin128 = jnp.tile(jnp.concatenate([-sin, sin], axis=1), (1, 4))

    tri_out = jnp.tril(jnp.ones((TM_OUT, TM_OUT), F32)).astype(BF16)

    h = x.reshape(t, d)
    for l in range(DEPTH):
        p = _pack_layer(w_in[l], gla_w_gate[l], gla_b_gate[l], gla_norm[l], pool_w[l], pool_scale[l],
                        diff_lq1[l], diff_lk1[l], diff_lq2[l], diff_lk2[l], diff_norm[l], mla_q_norm[l],
                        mla_w_uq[l], mla_kv_norm[l], mla_w_ukv[l], router_w[l], router_b[l])
        gla_in, pool_in, dq, dk, dv, mq, mk, mv = _proj(
            h, p['w_big'], cos128, sin128, p['qn'], p['kvn'], p['wuq'], p['wukv'], p['sel'])
        gla_out = _gla(gla_in, p['wg'], p['bg'], p['gnorm'], p['ind'], p['tri'], batch)
        pool_out = _pool(pool_in, p['pool_bd'], p['pscale'], batch)
        lam_init = 0.8 - 0.6 * math.exp(-0.3 * l)
        diff_out = _flash(dq, dk, dv, p['aux'], batch, DIFF_HEADS_SPEC, lam_init)
        mla_out = _flash(mq, mk, mv, p['aux'], batch, MLA_HEADS_SPEC, None)
        h1, top_idx, gates, rank, tile_cnt = _outproj(
            (gla_out, pool_out, diff_out, mla_out), h, w_out[l].astype(BF16), ln1_g[l].reshape(1, d),
            ln1_b[l].reshape(1, d), p['rwh'], p['rwl'], p['rb'], tri_out)
        dest, block_e, nb, cnt = _route(top_idx[:, :TOP_K], rank[:, :TOP_K], tile_cnt[:, 7, :], BM_MOE, TM_OUT)
        n_rows = t * TOP_K + N_EXPERTS * BM_MOE
        xs = _dispatch(h1, dest, cnt, n_rows)
        ys = _experts(xs, block_e, nb, w_gate_up, b_gate_up.reshape(DEPTH, N_EXPERTS, 1, 2 * D_EXPERT),
                      w_down, b_down.reshape(DEPTH, N_EXPERTS, 1, D_MODEL), l)
        h = _combine(ys, dest, gates, h1, ln2_g[l].reshape(1, d), ln2_b[l].reshape(1, d))
    return h.reshape(batch, seq, d)
```
